```python
import jax, jax.numpy as jnp
from jax import lax
import numpy as np

D_MODEL = 2048
BATCH = 2
SEQ = 4096
DEPTH = 1

HEAD_DIM = 128
MIX_WIDTH = D_MODEL
A_WIDTH = MIX_WIDTH // 2
B_WIDTH = MIX_WIDTH - A_WIDTH
A_HEADS = A_WIDTH // HEAD_DIM
B_SUB_DIM = HEAD_DIM // 2
B_HEADS = B_WIDTH // (2 * B_SUB_DIM)
B_VDIM = 2 * B_SUB_DIM
IN_WIDTH = 3 * A_WIDTH + 3 * B_WIDTH
D_FF = 5632
ROPE_THETA = 500000.0
ROPE_FRACTION = 4
DILATED_PATTERNS = ((128, 1), (512, 4), (2048, 16))
Q_BLOCK = 128
EPS = 1e-6

kernel_name = "hybrid_dilated_diff_macaron_block"


def rms_norm(x, g):
    x32 = x.astype(jnp.float32)
    y = x32 * lax.rsqrt(jnp.mean(x32 * x32, axis=-1, keepdims=True) + EPS)
    return (y * g.astype(jnp.float32)).astype(x.dtype)


def swiglu_ffn(x, g, w_in, w_out):
    h = rms_norm(x, g)
    gate, up = jnp.split(h @ w_in, 2, axis=-1)
    return (jax.nn.silu(gate) * up) @ w_out


def partial_rope(x):
    S, hd = x.shape[1], x.shape[-1]
    rd = hd // ROPE_FRACTION
    half = rd // 2
    inv = ROPE_THETA ** (-jnp.arange(0, rd, 2, dtype=jnp.float32) / rd)
    ang = jnp.arange(S, dtype=jnp.float32)[:, None] * inv[None, :]
    bshape = (S,) + (1,) * (x.ndim - 3) + (half,)
    cos = jnp.cos(ang).reshape(bshape).astype(x.dtype)
    sin = jnp.sin(ang).reshape(bshape).astype(x.dtype)
    x1, x2, xp = x[..., :half], x[..., half:rd], x[..., rd:]
    return jnp.concatenate([x1 * cos - x2 * sin, x2 * cos + x1 * sin, xp], axis=-1)


def dilated_offsets():
    offs = []
    for window, dil in DILATED_PATTERNS:
        half = window // (2 * dil)
        offs.append(dil * jnp.arange(-half, half + 1, dtype=jnp.int32))
    return offs


def dilated_mixture_attention(q, k, v):
    B, H, S, hd = q.shape
    scale = hd ** -0.5
    offsets = dilated_offsets()

    def block(i):
        start = i * Q_BLOCK
        qb = lax.dynamic_slice_in_dim(q, start, Q_BLOCK, axis=2)
        pos = start + jnp.arange(Q_BLOCK, dtype=jnp.int32)
        outs, lses = [], []
        for offs in offsets:
            kpos = pos[:, None] + offs[None, :]
            valid = (kpos >= 0) & (kpos < S)
            idx = jnp.clip(kpos, 0, S - 1)
            kg = k[:, :, idx]
            vg = v[:, :, idx]
            s = jnp.einsum('bhqd,bhqkd->bhqk', qb, kg).astype(jnp.float32) * scale
            s = jnp.where(valid[None, None], s, -jnp.inf)
            lse = jax.nn.logsumexp(s, axis=-1)
            p = jnp.exp(s - lse[..., None])
            outs.append(jnp.einsum('bhqk,bhqkd->bhqd', p.astype(v.dtype), vg))
            lses.append(lse)
        o = jnp.stack(outs, axis=3)
        w = jax.nn.softmax(jnp.stack(lses, axis=-1), axis=-1)
        return jnp.einsum('bhqg,bhqgd->bhqd', w.astype(v.dtype), o)

    out = lax.map(block, jnp.arange(S // Q_BLOCK))
    return out.transpose(1, 2, 0, 3, 4).reshape(B, H, S, hd)


def differential_attention(q, k, v, lam):
    B, H, S = q.shape[:3]
    scale = q.shape[-1] ** -0.5

    def block(i):
        qb = lax.dynamic_slice_in_dim(q, i * Q_BLOCK, Q_BLOCK, axis=2)
        s = jnp.einsum('bhqcd,bhkcd->bhcqk', qb, k).astype(jnp.float32) * scale
        p = jax.nn.softmax(s, axis=-1)
        a = p[:, :, 0] - lam * p[:, :, 1]
        return jnp.einsum('bhqk,bhkd->bhqd', a.astype(v.dtype), v)

    out = lax.map(block, jnp.arange(S // Q_BLOCK))
    return out.transpose(1, 2, 0, 3, 4).reshape(B, H, S, v.shape[-1])


def hybrid_layer(x, layer_idx, ffn1_norm, ffn1_w_in, ffn1_w_out, mix_norm, w_in,
                 a_q_norm, a_k_norm, b_q_norm, b_k_norm,
                 lambda_q1, lambda_k1, lambda_q2, lambda_k2,
                 a_out_norm, b_out_norm, w_out, ffn2_norm, ffn2_w_in, ffn2_w_out):
    B, S, _ = x.shape
    x = x + 0.5 * swiglu_ffn(x, ffn1_norm, ffn1_w_in, ffn1_w_out)

    h = rms_norm(x, mix_norm)
    proj = h @ w_in
    a_q, a_k, a_v, b_q, b_k, b_v = jnp.split(
        proj, np.cumsum([A_WIDTH] * 3 + [B_WIDTH] * 2).tolist(), axis=-1)

    a_q = partial_rope(rms_norm(a_q.reshape(B, S, A_HEADS, HEAD_DIM), a_q_norm))
    a_k = partial_rope(rms_norm(a_k.reshape(B, S, A_HEADS, HEAD_DIM), a_k_norm))
    a_v = a_v.reshape(B, S, A_HEADS, HEAD_DIM)
    a_o = dilated_mixture_attention(a_q.transpose(0, 2, 1, 3), a_k.transpose(0, 2, 1, 3),
                                    a_v.transpose(0, 2, 1, 3))
    a_o = rms_norm(a_o, a_out_norm)

    b_q = partial_rope(rms_norm(b_q.reshape(B, S, B_HEADS, 2, B_SUB_DIM), b_q_norm))
    b_k = partial_rope(rms_norm(b_k.reshape(B, S, B_HEADS, 2, B_SUB_DIM), b_k_norm))
    b_v = b_v.reshape(B, S, B_HEADS, B_VDIM)
    lambda_init = 0.8 - 0.6 * float(np.exp(-0.3 * layer_idx))
    lam = (jnp.exp(jnp.sum(lambda_q1.astype(jnp.float32) * lambda_k1.astype(jnp.float32)))
           - jnp.exp(jnp.sum(lambda_q2.astype(jnp.float32) * lambda_k2.astype(jnp.float32)))
           + lambda_init)
    b_o = differential_attention(b_q.transpose(0, 2, 1, 3, 4), b_k.transpose(0, 2, 1, 3, 4),
                                 b_v.transpose(0, 2, 1, 3), lam)
    b_o = rms_norm(b_o, b_out_norm) * (1.0 - lambda_init)

    mixed = jnp.concatenate([a_o.transpose(0, 2, 1, 3).reshape(B, S, A_WIDTH),
                             b_o.transpose(0, 2, 1, 3).reshape(B, S, B_WIDTH)], axis=-1)
    x = x + mixed @ w_out

    x = x + 0.5 * swiglu_ffn(x, ffn2_norm, ffn2_w_in, ffn2_w_out)
    return x


def setup_inputs(seed: int = 0) -> dict:
    key = jax.random.key(seed)
    ks = jax.random.split(key, 20)
    f32 = jnp.float32

    def nrm(k, shape, scale):
        return jax.random.normal(k, shape, f32) * scale

    def gain(k, shape):
        return 1.0 + 0.02 * jax.random.normal(k, shape, f32)

    return {
        "x": jax.random.normal(ks[0], (BATCH, SEQ, D_MODEL), f32),
        "ffn1_norm": gain(ks[1], (DEPTH, D_MODEL)),
        "ffn1_w_in": nrm(ks[2], (DEPTH, D_MODEL, 2 * D_FF), D_MODEL ** -0.5),
        "ffn1_w_out": nrm(ks[3], (DEPTH, D_FF, D_MODEL), D_FF ** -0.5),
        "mix_norm": gain(ks[4], (DEPTH, D_MODEL)),
        "w_in": nrm(ks[5], (DEPTH, D_MODEL, IN_WIDTH), D_MODEL ** -0.5),
        "a_q_norm": gain(ks[6], (DEPTH, HEAD_DIM)),
        "a_k_norm": gain(ks[7], (DEPTH, HEAD_DIM)),
        "b_q_norm": gain(ks[8], (DEPTH, B_SUB_DIM)),
        "b_k_norm": gain(ks[9], (DEPTH, B_SUB_DIM)),
        "lambda_q1": nrm(ks[10], (DEPTH, B_SUB_DIM), 0.1),
        "lambda_k1": nrm(ks[11], (DEPTH, B_SUB_DIM), 0.1),
        "lambda_q2": nrm(ks[12], (DEPTH, B_SUB_DIM), 0.1),
        "lambda_k2": nrm(ks[13], (DEPTH, B_SUB_DIM), 0.1),
        "a_out_norm": gain(ks[14], (DEPTH, HEAD_DIM)),
        "b_out_norm": gain(ks[15], (DEPTH, B_VDIM)),
        "w_out": nrm(ks[16], (DEPTH, MIX_WIDTH, D_MODEL), MIX_WIDTH ** -0.5),
        "ffn2_norm": gain(ks[17], (DEPTH, D_MODEL)),
        "ffn2_w_in": nrm(ks[18], (DEPTH, D_MODEL, 2 * D_FF), D_MODEL ** -0.5),
        "ffn2_w_out": nrm(ks[19], (DEPTH, D_FF, D_MODEL), D_FF ** -0.5),
    }


def reference(x, ffn1_norm, ffn1_w_in, ffn1_w_out, mix_norm, w_in,
              a_q_norm, a_k_norm, b_q_norm, b_k_norm,
              lambda_q1, lambda_k1, lambda_q2, lambda_k2,
              a_out_norm, b_out_norm, w_out, ffn2_norm, ffn2_w_in, ffn2_w_out):
    for l in range(DEPTH):
        x = hybrid_layer(x, l, ffn1_norm[l], ffn1_w_in[l], ffn1_w_out[l], mix_norm[l], w_in[l],
                         a_q_norm[l], a_k_norm[l], b_q_norm[l], b_k_norm[l],
                         lambda_q1[l], lambda_k1[l], lambda_q2[l], lambda_k2[l],
                         a_out_norm[l], b_out_norm[l], w_out[l],
                         ffn2_norm[l], ffn2_w_in[l], ffn2_w_out[l])
    return x
```

```python
import functools
import math

import numpy as np
import jax
import jax.numpy as jnp
from jax import lax
from jax.experimental import pallas as pl
from jax.experimental.pallas import tpu as pltpu

F32 = jnp.float32
BF16 = jnp.bfloat16

HEAD_DIM = 128
N_HEADS = 8
GROUP_WIDTH = N_HEADS * HEAD_DIM
B_SUB_DIM = 64
ROPE_THETA = 500000.0
ROPE_FRACTION = 4
PATTERNS = ((128, 1), (512, 4), (2048, 16))
HALF_WIN = 64
EPS = 1e-6
NEG = -1e30

Q_BLK = 128
K_BLK = Q_BLK + 2 * HALF_WIN

VMEM_LIMIT = 56 * 1024 * 1024


def _cparams(sem):
    return pltpu.CompilerParams(dimension_semantics=sem, vmem_limit_bytes=VMEM_LIMIT)


def _ffn_kernel(x_ref, g_ref, wg_ref, wu_ref, wo_ref, o_ref, h_ref):
    j = pl.program_id(1)

    @pl.when(j == 0)
    def _():
        x = x_ref[...]
        ms = jnp.mean(x * x, axis=-1, keepdims=True)
        h_ref[...] = (x * lax.rsqrt(ms + EPS) * g_ref[...]).astype(BF16)
        o_ref[...] = x

    h = h_ref[...]
    gate = jnp.dot(h, wg_ref[...], preferred_element_type=F32)
    up = jnp.dot(h, wu_ref[...], preferred_element_type=F32)
    act = (gate * jax.nn.sigmoid(gate) * up * 0.5).astype(BF16)
    o_ref[...] += jnp.dot(act, wo_ref[...], preferred_element_type=F32)


def _ffn(x2d, gain, w_in, w_out, *, tm=512, tf=512):
    T, D = x2d.shape
    d_ff = w_out.shape[0]
    nj = d_ff // tf
    return pl.pallas_call(
        _ffn_kernel,
        grid=(T // tm, nj),
        in_specs=[
            pl.BlockSpec((tm, D), lambda i, j: (i, 0)),
            pl.BlockSpec((1, D), lambda i, j: (0, 0)),
            pl.BlockSpec((D, tf), lambda i, j: (0, j)),
            pl.BlockSpec((D, tf), lambda i, j: (0, j + nj)),
            pl.BlockSpec((tf, D), lambda i, j: (j, 0)),
        ],
        out_specs=pl.BlockSpec((tm, D), lambda i, j: (i, 0)),
        out_shape=jax.ShapeDtypeStruct((T, D), F32),
        scratch_shapes=[pltpu.VMEM((tm, D), BF16)],
        compiler_params=_cparams(("parallel", "arbitrary")),
        name="ffn",
    )(x2d, gain.reshape(1, D), w_in, w_in, w_out)


def _rope_tables(seq, sub_dim):
    rd = sub_dim // ROPE_FRACTION
    half = rd // 2
    inv = ROPE_THETA ** (-np.arange(0, rd, 2, dtype=np.float64) / rd)
    ang = np.arange(seq, dtype=np.float64)[:, None] * inv[None, :]
    cos, sin = np.cos(ang), np.sin(ang)
    c = np.ones((seq, sub_dim))
    s = np.zeros((seq, sub_dim))
    c[:, :half] = cos
    c[:, half:rd] = cos
    s[:, :half] = -sin
    s[:, half:rd] = sin
    reps = HEAD_DIM // sub_dim
    return (np.tile(c, (1, reps)).astype(np.float32), np.tile(s, (1, reps)).astype(np.float32))


def _inproj_kernel(x_ref, g_ref, w_ref, cos_ref, sin_ref, gain_ref, o_ref, h_ref, *,
                   sub_dim, q_scale):
    j = pl.program_id(1)

    @pl.when(j == 0)
    def _():
        x = x_ref[...]
        ms = jnp.mean(x * x, axis=-1, keepdims=True)
        h_ref[...] = (x * lax.rsqrt(ms + EPS) * g_ref[...]).astype(BF16)

    p = jnp.dot(h_ref[...], w_ref[...], preferred_element_type=F32)
    half = sub_dim // ROPE_FRACTION // 2

    @pl.when(j < 2)
    def _():
        lane = lax.broadcasted_iota(jnp.int32, (1, HEAD_DIM), 1)
        sub_lane = lane % sub_dim
        cos = cos_ref[...]
        sin = sin_ref[...]
        gain = gain_ref[0] * jnp.where(j == 0, q_scale, 1.0)
        for hd in range(N_HEADS):
            ph = p[:, hd * HEAD_DIM:(hd + 1) * HEAD_DIM]
            sq = ph * ph
            if sub_dim == HEAD_DIM:
                ms = jnp.sum(sq, axis=-1, keepdims=True) * (1.0 / sub_dim)
                inv = lax.rsqrt(ms + EPS)
            else:
                lo = lane < sub_dim
                ms_lo = jnp.sum(jnp.where(lo, sq, 0.0), axis=-1, keepdims=True) * (1.0 / sub_dim)
                ms_hi = jnp.sum(jnp.where(lo, 0.0, sq), axis=-1, keepdims=True) * (1.0 / sub_dim)
                inv = jnp.where(lo, lax.rsqrt(ms_lo + EPS), lax.rsqrt(ms_hi + EPS))
            y = ph * inv * gain
            rot = jnp.where(sub_lane < half, pltpu.roll(y, HEAD_DIM - half, 1),
                            pltpu.roll(y, half, 1))
            o_ref[hd] = (y * cos + rot * sin).astype(o_ref.dtype)

    @pl.when(j == 2)
    def _():
        for hd in range(N_HEADS):
            o_ref[hd] = p[:, hd * HEAD_DIM:(hd + 1) * HEAD_DIM].astype(o_ref.dtype)


def _inproj(x2d, mix_gain, w_in, col0, q_gain, k_gain, *, batch, seq, sub_dim, out_dtype, tm=512):
    T, D = x2d.shape
    spb = seq // tm
    cos, sin = _rope_tables(seq, sub_dim)
    reps = HEAD_DIM // sub_dim
    gains = jnp.stack([jnp.tile(q_gain, reps), jnp.tile(k_gain, reps)]).reshape(2, 1, HEAD_DIM)
    cb0 = col0 // GROUP_WIDTH
    kern = functools.partial(_inproj_kernel, sub_dim=sub_dim, q_scale=sub_dim ** -0.5)
    return pl.pallas_call(
        kern,
        grid=(T // tm, 3),
        in_specs=[
            pl.BlockSpec((tm, D), lambda i, j: (i, 0)),
            pl.BlockSpec((1, D), lambda i, j: (0, 0)),
            pl.BlockSpec((D, GROUP_WIDTH), lambda i, j: (0, cb0 + j)),
            pl.BlockSpec((tm, HEAD_DIM), lambda i, j: (i % spb, 0)),
            pl.BlockSpec((tm, HEAD_DIM), lambda i, j: (i % spb, 0)),
            pl.BlockSpec((1, 1, HEAD_DIM), lambda i, j: (jnp.minimum(j, 1), 0, 0)),
        ],
        out_specs=pl.BlockSpec((None, None, N_HEADS, tm, HEAD_DIM),
                               lambda i, j: (j, i // spb, 0, i % spb, 0)),
        out_shape=jax.ShapeDtypeStruct((3, batch, N_HEADS, seq, HEAD_DIM), out_dtype),
        scratch_shapes=[pltpu.VMEM((tm, D), BF16)],
        compiler_params=_cparams(("parallel", "arbitrary")),
        name="inproj_%d" % sub_dim,
    )(x2d, mix_gain.reshape(1, D), w_in, jnp.asarray(cos), jnp.asarray(sin), gains)


def _dilated_kernel(q_ref, k_ref, v_ref, g_ref, o_ref, og_ref, lg_ref, *, seq):
    col_minus_row = (lax.broadcasted_iota(jnp.int32, (Q_BLK, K_BLK), 1)
                     - lax.broadcasted_iota(jnp.int32, (Q_BLK, K_BLK), 0))

    for g, (_, dil) in enumerate(PATTERNS):
        sub_len = seq // dil
        nblk = sub_len // Q_BLK

        def body(t, carry, g=g, dil=dil, sub_len=sub_len, nblk=nblk):
            r = t // nblk
            m0 = (t % nblk) * Q_BLK
            ks = jnp.clip(m0 - HALF_WIN, 0, sub_len - K_BLK)
            if dil == 1:
                qs = pl.ds(pl.multiple_of(m0, Q_BLK), Q_BLK)
                kv = pl.ds(pl.multiple_of(ks, 8), K_BLK)
            else:
                qs = pl.ds(r + dil * m0, Q_BLK, stride=dil)
                kv = pl.ds(r + dil * ks, K_BLK, stride=dil)
            q = q_ref[qs, :].astype(BF16)
            k = k_ref[kv, :].astype(BF16)
            v = v_ref[kv, :].astype(BF16)
            s = lax.dot_general(q, k, (((1,), (1,)), ((), ())), preferred_element_type=F32)
            valid = jnp.abs(col_minus_row + (ks - m0)) <= HALF_WIN
            s = jnp.where(valid, s, NEG)
            m = jnp.max(s, axis=-1, keepdims=True)
            p = jnp.exp(s - m)
            l = jnp.sum(p, axis=-1, keepdims=True)
            o = jnp.dot(p.astype(BF16), v, preferred_element_type=F32) / l
            og_ref[g, qs, :] = o
            lg_ref[g, qs, :] = jnp.broadcast_to(m + jnp.log(l), (Q_BLK, HEAD_DIM))
            return carry

        lax.fori_loop(0, dil * nblk, body, 0)

    chunk = 512

    def comb(c, carry):
        rows = pl.ds(pl.multiple_of(c * chunk, chunk), chunk)
        l0, l1, l2 = lg_ref[0, rows, :], lg_ref[1, rows, :], lg_ref[2, rows, :]
        m = jnp.maximum(jnp.maximum(l0, l1), l2)
        w0, w1, w2 = jnp.exp(l0 - m), jnp.exp(l1 - m), jnp.exp(l2 - m)
        o = (w0 * og_ref[0, rows, :] + w1 * og_ref[1, rows, :] + w2 * og_ref[2, rows, :]) / (w0 + w1 + w2)
        ms = jnp.mean(o * o, axis=-1, keepdims=True)
        o_ref[rows, :] = (o * lax.rsqrt(ms + EPS) * g_ref[...]).astype(o_ref.dtype)
        return carry

    lax.fori_loop(0, seq // chunk, comb, 0)


def _dilated(qkv, out_gain):
    _, batch, nh, seq, hd = qkv.shape
    spec = lambda which: pl.BlockSpec((None, None, None, seq, hd), lambda b, h: (which, b, h, 0, 0))
    return pl.pallas_call(
        functools.partial(_dilated_kernel, seq=seq),
        grid=(batch, nh),
        in_specs=[spec(0), spec(1), spec(2), pl.BlockSpec((1, hd), lambda b, h: (0, 0))],
        out_specs=pl.BlockSpec((None, seq, hd), lambda b, h: (b, 0, h)),
        out_shape=jax.ShapeDtypeStruct((batch, seq, nh * hd), BF16),
        scratch_shapes=[pltpu.VMEM((len(PATTERNS), seq, hd), F32),
                        pltpu.VMEM((len(PATTERNS), seq, hd), F32)],
        compiler_params=_cparams(("parallel", "parallel")),
        name="dilated",
    )(qkv, qkv, qkv, out_gain.reshape(1, hd))


def _diff_kernel(lam_ref, q_ref, k_ref, v_ref, g_ref, o_ref, *, out_scale, lambda_init):
    lp = lam_ref[...]
    lam = (jnp.exp(jnp.sum(lp[0:1] * lp[1:2], axis=-1, keepdims=True))
           - jnp.exp(jnp.sum(lp[2:3] * lp[3:4], axis=-1, keepdims=True)) + lambda_init)
    q = q_ref[...]
    k = k_ref[...]
    lane = lax.broadcasted_iota(jnp.int32, (1, HEAD_DIM), 1)
    zero = jnp.zeros_like(q)
    probs = []
    for c in range(2):
        in_map = (lane >= c * B_SUB_DIM) & (lane < (c + 1) * B_SUB_DIM)
        qc = jnp.where(in_map, q, zero)
        s = lax.dot_general(qc, k, (((1,), (1,)), ((), ())), preferred_element_type=F32)
        m = jnp.max(s, axis=-1, keepdims=True)
        p = jnp.exp(s - m)
        probs.append((p, jnp.sum(p, axis=-1, keepdims=True)))
    (p0, l0), (p1, l1) = probs
    a = p0 * (1.0 / l0) - p1 * (lam / l1)
    o = jnp.dot(a.astype(BF16), v_ref[...], preferred_element_type=F32)
    ms = jnp.mean(o * o, axis=-1, keepdims=True)
    o_ref[...] = (o * lax.rsqrt(ms + EPS) * (g_ref[...] * out_scale)).astype(o_ref.dtype)


def _diff(qkv, lam_params, out_gain, *, lambda_init, tq=256):
    _, batch, nh, seq, hd = qkv.shape
    kern = functools.partial(_diff_kernel, out_scale=1.0 - lambda_init, lambda_init=lambda_init)
    return pl.pallas_call(
        kern,
        grid=(batch, nh, seq // tq),
        in_specs=[
            pl.BlockSpec((4, B_SUB_DIM), lambda b, h, i: (0, 0)),
            pl.BlockSpec((None, None, None, tq, hd), lambda b, h, i: (0, b, h, i, 0)),
            pl.BlockSpec((None, None, None, seq, hd), lambda b, h, i: (1, b, h, 0, 0)),
            pl.BlockSpec((None, None, None, seq, hd), lambda b, h, i: (2, b, h, 0, 0)),
            pl.BlockSpec((1, hd), lambda b, h, i: (0, 0)),
        ],
        out_specs=pl.BlockSpec((None, tq, hd), lambda b, h, i: (b, i, h)),
        out_shape=jax.ShapeDtypeStruct((batch, seq, nh * hd), BF16),
        compiler_params=_cparams(("parallel", "parallel", "arbitrary")),
        name="diff",
    )(lam_params, qkv, qkv, qkv, out_gain.reshape(1, hd))


def _outproj_kernel(x_ref, a_ref, b_ref, wa_ref, wb_ref, o_ref):
    o_ref[...] = (x_ref[...]
                  + jnp.dot(a_ref[...], wa_ref[...], preferred_element_type=F32)
                  + jnp.dot(b_ref[...], wb_ref[...], preferred_element_type=F32))


def _outproj(x2d, a2d, b2d, w_out, *, tm=512):
    T, D = x2d.shape
    W = a2d.shape[1]
    return pl.pallas_call(
        _outproj_kernel,
        grid=(T // tm,),
        in_specs=[
            pl.BlockSpec((tm, D), lambda i: (i, 0)),
            pl.BlockSpec((tm, W), lambda i: (i, 0)),
            pl.BlockSpec((tm, W), lambda i: (i, 0)),
            pl.BlockSpec((W, D), lambda i: (0, 0)),
            pl.BlockSpec((W, D), lambda i: (1, 0)),
        ],
        out_specs=pl.BlockSpec((tm, D), lambda i: (i, 0)),
        out_shape=jax.ShapeDtypeStruct((T, D), F32),
        compiler_params=_cparams(("parallel",)),
        name="outproj",
    )(x2d, a2d, b2d, w_out, w_out)


def _layer(x, layer_idx, ffn1_norm, ffn1_w_in, ffn1_w_out, mix_norm, w_in,
           a_q_norm, a_k_norm, b_q_norm, b_k_norm,
           lambda_q1, lambda_k1, lambda_q2, lambda_k2,
           a_out_norm, b_out_norm, w_out, ffn2_norm, ffn2_w_in, ffn2_w_out):
    batch, seq, d_model = x.shape
    x2d = x.reshape(batch * seq, d_model)
    lambda_init = 0.8 - 0.6 * math.exp(-0.3 * layer_idx)

    x1 = _ffn(x2d, ffn1_norm, ffn1_w_in.astype(BF16), ffn1_w_out.astype(BF16))

    w_in16 = w_in.astype(BF16)
    qkv_a = _inproj(x1, mix_norm, w_in16, 0, a_q_norm, a_k_norm,
                    batch=batch, seq=seq, sub_dim=HEAD_DIM, out_dtype=F32)
    qkv_b = _inproj(x1, mix_norm, w_in16, 3 * GROUP_WIDTH, b_q_norm, b_k_norm,
                    batch=batch, seq=seq, sub_dim=B_SUB_DIM, out_dtype=BF16)

    a_o = _dilated(qkv_a, a_out_norm)
    lam_params = jnp.stack([lambda_q1, lambda_k1, lambda_q2, lambda_k2]).astype(F32)
    b_o = _diff(qkv_b, lam_params, b_out_norm, lambda_init=lambda_init)

    x2 = _outproj(x1, a_o.reshape(batch * seq, GROUP_WIDTH), b_o.reshape(batch * seq, GROUP_WIDTH),
                  w_out.astype(BF16))
    out = _ffn(x2, ffn2_norm, ffn2_w_in.astype(BF16), ffn2_w_out.astype(BF16))
    return out.reshape(batch, seq, d_model)


def kernel(x, ffn1_norm, ffn1_w_in, ffn1_w_out, mix_norm, w_in, a_q_norm, a_k_norm, b_q_norm, b_k_norm,
           lambda_q1, lambda_k1, lambda_q2, lambda_k2, a_out_norm, b_out_norm, w_out,
           ffn2_norm, ffn2_w_in, ffn2_w_out):
    for l in range(ffn1_norm.shape[0]):
        x = _layer(x, l, ffn1_norm[l], ffn1_w_in[l], ffn1_w_out[l], mix_norm[l], w_in[l],
                   a_q_norm[l], a_k_norm[l], b_q_norm[l], b_k_norm[l],
                   lambda_q1[l], lambda_k1[l], lambda_q2[l], lambda_k2[l],
                   a_out_norm[l], b_out_norm[l], w_out[l],
                   ffn2_norm[l], ffn2_w_in[l], ffn2_w_out[l])
    return x
```

```python
import functools
import math

import numpy as np
import jax
import jax.numpy as jnp
from jax import lax
from jax.experimental import pallas as pl
from jax.experimental.pallas import tpu as pltpu

F32 = jnp.float32
BF16 = jnp.bfloat16

HEAD_DIM = 128
N_HEADS = 8
GROUP_WIDTH = N_HEADS * HEAD_DIM
B_SUB_DIM = 64
ROPE_THETA = 500000.0
ROPE_FRACTION = 4
PATTERNS = ((128, 1), (512, 4), (2048, 16))
HALF_WIN = 64
EPS = 1e-6
NEG = -1e30
LOG2E = math.log2(math.e)

Q_BLK = 128
K_BLK = Q_BLK + 2 * HALF_WIN

VMEM_LIMIT = 56 * 1024 * 1024


def _cparams(sem):
    return pltpu.CompilerParams(dimension_semantics=sem, vmem_limit_bytes=VMEM_LIMIT)


def _ffn_kernel(x_ref, g_ref, wg_ref, wu_ref, wo_ref, o_ref, h_ref):
    j = pl.program_id(1)

    @pl.when(j == 0)
    def _():
        x = x_ref[...]
        ms = jnp.mean(x * x, axis=-1, keepdims=True)
        h_ref[...] = (x * lax.rsqrt(ms + EPS) * g_ref[...]).astype(BF16)
        o_ref[...] = x

    h = h_ref[...]
    gate = jnp.dot(h, wg_ref[...], preferred_element_type=F32)
    up = jnp.dot(h, wu_ref[...], preferred_element_type=F32)
    act = (gate * jax.nn.sigmoid(gate) * up * 0.5).astype(BF16)
    o_ref[...] += jnp.dot(act, wo_ref[...], preferred_element_type=F32)


def _ffn(x2d, gain, w_in, w_out, *, tm=512, tf=512):
    T, D = x2d.shape
    d_ff = w_out.shape[0]
    nj = d_ff // tf
    return pl.pallas_call(
        _ffn_kernel,
        grid=(T // tm, nj),
        in_specs=[
            pl.BlockSpec((tm, D), lambda i, j: (i, 0)),
            pl.BlockSpec((1, D), lambda i, j: (0, 0)),
            pl.BlockSpec((D, tf), lambda i, j: (0, j)),
            pl.BlockSpec((D, tf), lambda i, j: (0, j + nj)),
            pl.BlockSpec((tf, D), lambda i, j: (j, 0)),
        ],
        out_specs=pl.BlockSpec((tm, D), lambda i, j: (i, 0)),
        out_shape=jax.ShapeDtypeStruct((T, D), F32),
        scratch_shapes=[pltpu.VMEM((tm, D), BF16)],
        compiler_params=_cparams(("parallel", "arbitrary")),
        name="ffn",
    )(x2d, gain.reshape(1, D), w_in, w_in, w_out)


def _rope_tables(seq, sub_dim):
    rd = sub_dim // ROPE_FRACTION
    half = rd // 2
    inv = ROPE_THETA ** (-np.arange(0, rd, 2, dtype=np.float64) / rd)
    ang = np.arange(seq, dtype=np.float64)[:, None] * inv[None, :]
    cos, sin = np.cos(ang), np.sin(ang)
    c = np.ones((seq, sub_dim))
    s = np.zeros((seq, sub_dim))
    c[:, :half] = cos
    c[:, half:rd] = cos
    s[:, :half] = -sin
    s[:, half:rd] = sin
    reps = HEAD_DIM // sub_dim
    return (np.tile(c, (1, reps)).astype(np.float32), np.tile(s, (1, reps)).astype(np.float32))


def _inproj_kernel(x_ref, g_ref, w_ref, cos_ref, sin_ref, gain_ref, o_ref, h_ref, *,
                   sub_dim, q_scale):
    j = pl.program_id(1)

    @pl.when(j == 0)
    def _():
        x = x_ref[...]
        ms = jnp.mean(x * x, axis=-1, keepdims=True)
        h_ref[...] = (x * lax.rsqrt(ms + EPS) * g_ref[...]).astype(BF16)

    p = jnp.dot(h_ref[...], w_ref[...], preferred_element_type=F32)
    half = sub_dim // ROPE_FRACTION // 2

    @pl.when(j < 2)
    def _():
        lane = lax.broadcasted_iota(jnp.int32, (1, HEAD_DIM), 1)
        sub_lane = lane % sub_dim
        cos = cos_ref[...]
        sin = sin_ref[...]
        gain = gain_ref[0] * jnp.where(j == 0, q_scale, 1.0)
        for hd in range(N_HEADS):
            ph = p[:, hd * HEAD_DIM:(hd + 1) * HEAD_DIM]
            sq = ph * ph
            if sub_dim == HEAD_DIM:
                ms = jnp.sum(sq, axis=-1, keepdims=True) * (1.0 / sub_dim)
                inv = lax.rsqrt(ms + EPS)
            else:
                lo = lane < sub_dim
                ms_lo = jnp.sum(jnp.where(lo, sq, 0.0), axis=-1, keepdims=True) * (1.0 / sub_dim)
                ms_hi = jnp.sum(jnp.where(lo, 0.0, sq), axis=-1, keepdims=True) * (1.0 / sub_dim)
                inv = jnp.where(lo, lax.rsqrt(ms_lo + EPS), lax.rsqrt(ms_hi + EPS))
            y = ph * inv * gain
            rot = jnp.where(sub_lane < half, pltpu.roll(y, HEAD_DIM - half, 1),
                            pltpu.roll(y, half, 1))
            o_ref[hd] = (y * cos + rot * sin).astype(o_ref.dtype)

    @pl.when(j == 2)
    def _():
        for hd in range(N_HEADS):
            o_ref[hd] = p[:, hd * HEAD_DIM:(hd + 1) * HEAD_DIM].astype(o_ref.dtype)


def _inproj(x2d, mix_gain, w_in, col0, q_gain, k_gain, *, batch, seq, sub_dim, out_dtype, tm=512):
    T, D = x2d.shape
    spb = seq // tm
    cos, sin = _rope_tables(seq, sub_dim)
    reps = HEAD_DIM // sub_dim
    gains = jnp.stack([jnp.tile(q_gain, reps), jnp.tile(k_gain, reps)]).reshape(2, 1, HEAD_DIM)
    cb0 = col0 // GROUP_WIDTH
    kern = functools.partial(_inproj_kernel, sub_dim=sub_dim, q_scale=LOG2E * sub_dim ** -0.5)
    return pl.pallas_call(
        kern,
        grid=(T // tm, 3),
        in_specs=[
            pl.BlockSpec((tm, D), lambda i, j: (i, 0)),
            pl.BlockSpec((1, D), lambda i, j: (0, 0)),
            pl.BlockSpec((D, GROUP_WIDTH), lambda i, j: (0, cb0 + j)),
            pl.BlockSpec((tm, HEAD_DIM), lambda i, j: (i % spb, 0)),
            pl.BlockSpec((tm, HEAD_DIM), lambda i, j: (i % spb, 0)),
            pl.BlockSpec((1, 1, HEAD_DIM), lambda i, j: (jnp.minimum(j, 1), 0, 0)),
        ],
        out_specs=pl.BlockSpec((None, None, N_HEADS, tm, HEAD_DIM),
                               lambda i, j: (j, i // spb, 0, i % spb, 0)),
        out_shape=jax.ShapeDtypeStruct((3, batch, N_HEADS, seq, HEAD_DIM), out_dtype),
        scratch_shapes=[pltpu.VMEM((tm, D), BF16)],
        compiler_params=_cparams(("parallel", "arbitrary")),
        name="inproj_%d" % sub_dim,
    )(x2d, mix_gain.reshape(1, D), w_in, jnp.asarray(cos), jnp.asarray(sin), gains)


def _band_bias():
    col_minus_row = np.arange(K_BLK)[None, :] - np.arange(Q_BLK)[:, None]
    return np.stack([np.where(np.abs(col_minus_row - lead) <= HALF_WIN, 0.0, NEG)
                     for lead in (0, HALF_WIN, 2 * HALF_WIN)]).astype(np.float32)


def _dilated_kernel(q_ref, k_ref, v_ref, bias_ref, g_ref, o_ref,
                    qs_ref, ks_ref, vs_ref, og_ref, lg_ref, *, seq):
    stage_rows = 256
    for g, (_, dil) in enumerate(PATTERNS):
        sub_len = seq // dil
        per_res = sub_len // stage_rows

        def stage(t, carry, g=g, dil=dil, sub_len=sub_len, per_res=per_res):
            r = t // per_res
            c0 = (t % per_res) * stage_rows
            dst = pl.ds(pl.multiple_of(r * sub_len + c0, stage_rows), stage_rows)
            if dil == 1:
                src = dst
            else:
                src = pl.ds(r + dil * c0, stage_rows, stride=dil)
            qs_ref[g, dst, :] = q_ref[src, :].astype(BF16)
            ks_ref[g, dst, :] = k_ref[src, :].astype(BF16)
            vs_ref[g, dst, :] = v_ref[src, :].astype(BF16)
            return carry

        lax.fori_loop(0, seq // stage_rows, stage, 0)

    unroll = 8
    for g, (_, dil) in enumerate(PATTERNS):
        sub_len = seq // dil
        nblk = sub_len // Q_BLK

        def body(it, carry, g=g, dil=dil, sub_len=sub_len, nblk=nblk):
            kvs, outs, scores = [], [], []
            for u in range(unroll):
                t = it * unroll + u
                r = t // nblk
                m0 = (t % nblk) * Q_BLK
                k0 = jnp.clip(m0 - HALF_WIN, 0, sub_len - K_BLK)
                base = r * sub_len
                q = qs_ref[g, pl.ds(pl.multiple_of(base + m0, Q_BLK), Q_BLK), :]
                kv = pl.ds(pl.multiple_of(base + k0, HALF_WIN), K_BLK)
                s = lax.dot_general(q, ks_ref[g, kv, :], (((1,), (1,)), ((), ())),
                                    preferred_element_type=F32)
                scores.append(s + bias_ref[(m0 - k0) // HALF_WIN])
                kvs.append(kv)
                if dil == 1:
                    outs.append(pl.ds(pl.multiple_of(m0, Q_BLK), Q_BLK))
                else:
                    outs.append(pl.ds(r + dil * m0, Q_BLK, stride=dil))
            s = jnp.concatenate(scores, axis=0)
            m = jnp.max(s, axis=-1, keepdims=True)
            p = jnp.exp2(s - m)
            l = jnp.sum(p, axis=-1, keepdims=True)
            p = p.astype(BF16)
            inv_l = 1.0 / l
            lse = jnp.broadcast_to(m + jnp.log2(l), (unroll * Q_BLK, HEAD_DIM))
            for u in range(unroll):
                blk = slice(u * Q_BLK, (u + 1) * Q_BLK)
                o = jnp.dot(p[blk], vs_ref[g, kvs[u], :], preferred_element_type=F32)
                og_ref[g, outs[u], :] = o * inv_l[blk]
                lg_ref[g, outs[u], :] = lse[blk]
            return carry

        lax.fori_loop(0, dil * nblk // unroll, body, 0)

    chunk = 256

    def comb(c, carry):
        rows = pl.ds(pl.multiple_of(c * chunk, chunk), chunk)
        l0, l1, l2 = lg_ref[0, rows, :], lg_ref[1, rows, :], lg_ref[2, rows, :]
        m = jnp.maximum(jnp.maximum(l0, l1), l2)
        w0, w1, w2 = jnp.exp2(l0 - m), jnp.exp2(l1 - m), jnp.exp2(l2 - m)
        o = (w0 * og_ref[0, rows, :] + w1 * og_ref[1, rows, :] + w2 * og_ref[2, rows, :]) / (w0 + w1 + w2)
        ms = jnp.mean(o * o, axis=-1, keepdims=True)
        o_ref[rows, :] = (o * lax.rsqrt(ms + EPS) * g_ref[...]).astype(o_ref.dtype)
        return carry

    lax.fori_loop(0, seq // chunk, comb, 0)


def _dilated(qkv, out_gain):
    _, batch, nh, seq, hd = qkv.shape
    npat = len(PATTERNS)
    spec = lambda which: pl.BlockSpec((None, None, None, seq, hd), lambda b, h: (which, b, h, 0, 0))
    return pl.pallas_call(
        functools.partial(_dilated_kernel, seq=seq),
        grid=(batch, nh),
        in_specs=[spec(0), spec(1), spec(2),
                  pl.BlockSpec((3, Q_BLK, K_BLK), lambda b, h: (0, 0, 0)),
                  pl.BlockSpec((1, hd), lambda b, h: (0, 0))],
        out_specs=pl.BlockSpec((None, seq, hd), lambda b, h: (b, 0, h)),
        out_shape=jax.ShapeDtypeStruct((batch, seq, nh * hd), BF16),
        scratch_shapes=[pltpu.VMEM((npat, seq, hd), BF16),
                        pltpu.VMEM((npat, seq, hd), BF16),
                        pltpu.VMEM((npat, seq, hd), BF16),
                        pltpu.VMEM((npat, seq, hd), F32),
                        pltpu.VMEM((npat, seq, hd), F32)],
        compiler_params=_cparams(("parallel", "parallel")),
        name="dilated",
    )(qkv, qkv, qkv, jnp.asarray(_band_bias()), out_gain.reshape(1, hd))


def _diff_kernel(lam_ref, q_ref, k_ref, v_ref, g_ref, o_ref, *, out_scale, lambda_init):
    lp = lam_ref[...]
    lam = (jnp.exp(jnp.sum(lp[0:1] * lp[1:2], axis=-1, keepdims=True))
           - jnp.exp(jnp.sum(lp[2:3] * lp[3:4], axis=-1, keepdims=True)) + lambda_init)
    q = q_ref[...]
    k = k_ref[...]
    lane = lax.broadcasted_iota(jnp.int32, (1, HEAD_DIM), 1)
    zero = jnp.zeros_like(q)
    probs = []
    for c in range(2):
        in_map = (lane >= c * B_SUB_DIM) & (lane < (c + 1) * B_SUB_DIM)
        qc = jnp.where(in_map, q, zero)
        s = lax.dot_general(qc, k, (((1,), (1,)), ((), ())), preferred_element_type=F32)
        m = jnp.max(s, axis=-1, keepdims=True)
        p = jnp.exp2(s - m)
        probs.append((p, jnp.sum(p, axis=-1, keepdims=True)))
    (p0, l0), (p1, l1) = probs
    a = p0 * (1.0 / l0) - p1 * (lam / l1)
    o = jnp.dot(a.astype(BF16), v_ref[...], preferred_element_type=F32)
    ms = jnp.mean(o * o, axis=-1, keepdims=True)
    o_ref[...] = (o * lax.rsqrt(ms + EPS) * (g_ref[...] * out_scale)).astype(o_ref.dtype)


def _diff(qkv, lam_params, out_gain, *, lambda_init, tq=256):
    _, batch, nh, seq, hd = qkv.shape
    kern = functools.partial(_diff_kernel, out_scale=1.0 - lambda_init, lambda_init=lambda_init)
    return pl.pallas_call(
        kern,
        grid=(batch, nh, seq // tq),
        in_specs=[
            pl.BlockSpec((4, B_SUB_DIM), lambda b, h, i: (0, 0)),
            pl.BlockSpec((None, None, None, tq, hd), lambda b, h, i: (0, b, h, i, 0)),
            pl.BlockSpec((None, None, None, seq, hd), lambda b, h, i: (1, b, h, 0, 0)),
            pl.BlockSpec((None, None, None, seq, hd), lambda b, h, i: (2, b, h, 0, 0)),
            pl.BlockSpec((1, hd), lambda b, h, i: (0, 0)),
        ],
        out_specs=pl.BlockSpec((None, tq, hd), lambda b, h, i: (b, i, h)),
        out_shape=jax.ShapeDtypeStruct((batch, seq, nh * hd), BF16),
        compiler_params=_cparams(("parallel", "parallel", "arbitrary")),
        name="diff",
    )(lam_params, qkv, qkv, qkv, out_gain.reshape(1, hd))


def _outproj_kernel(x_ref, a_ref, b_ref, wa_ref, wb_ref, o_ref):
    o_ref[...] = (x_ref[...]
                  + jnp.dot(a_ref[...], wa_ref[...], preferred_element_type=F32)
                  + jnp.dot(b_ref[...], wb_ref[...], preferred_element_type=F32))


def _outproj(x2d, a2d, b2d, w_out, *, tm=512):
    T, D = x2d.shape
    W = a2d.shape[1]
    return pl.pallas_call(
        _outproj_kernel,
        grid=(T // tm,),
        in_specs=[
            pl.BlockSpec((tm, D), lambda i: (i, 0)),
            pl.BlockSpec((tm, W), lambda i: (i, 0)),
            pl.BlockSpec((tm, W), lambda i: (i, 0)),
            pl.BlockSpec((W, D), lambda i: (0, 0)),
            pl.BlockSpec((W, D), lambda i: (1, 0)),
        ],
        out_specs=pl.BlockSpec((tm, D), lambda i: (i, 0)),
        out_shape=jax.ShapeDtypeStruct((T, D), F32),
        compiler_params=_cparams(("parallel",)),
        name="outproj",
    )(x2d, a2d, b2d, w_out, w_out)


def _layer(x, layer_idx, ffn1_norm, ffn1_w_in, ffn1_w_out, mix_norm, w_in,
           a_q_norm, a_k_norm, b_q_norm, b_k_norm,
           lambda_q1, lambda_k1, lambda_q2, lambda_k2,
           a_out_norm, b_out_norm, w_out, ffn2_norm, ffn2_w_in, ffn2_w_out):
    batch, seq, d_model = x.shape
    x2d = x.reshape(batch * seq, d_model)
    lambda_init = 0.8 - 0.6 * math.exp(-0.3 * layer_idx)

    x1 = _ffn(x2d, ffn1_norm, ffn1_w_in.astype(BF16), ffn1_w_out.astype(BF16))

    w_in16 = w_in.astype(BF16)
    qkv_a = _inproj(x1, mix_norm, w_in16, 0, a_q_norm, a_k_norm,
                    batch=batch, seq=seq, sub_dim=HEAD_DIM, out_dtype=F32)
    qkv_b = _inproj(x1, mix_norm, w_in16, 3 * GROUP_WIDTH, b_q_norm, b_k_norm,
                    batch=batch, seq=seq, sub_dim=B_SUB_DIM, out_dtype=BF16)

    a_o = _dilated(qkv_a, a_out_norm)
    lam_params = jnp.stack([lambda_q1, lambda_k1, lambda_q2, lambda_k2])
    b_o = _diff(qkv_b, lam_params, b_out_norm, lambda_init=lambda_init)

    x2 = _outproj(x1, a_o.reshape(batch * seq, GROUP_WIDTH), b_o.reshape(batch * seq, GROUP_WIDTH),
                  w_out.astype(BF16))
    out = _ffn(x2, ffn2_norm, ffn2_w_in.astype(BF16), ffn2_w_out.astype(BF16))
    return out.reshape(batch, seq, d_model)


def kernel(x, ffn1_norm, ffn1_w_in, ffn1_w_out, mix_norm, w_in, a_q_norm, a_k_norm, b_q_norm, b_k_norm,
           lambda_q1, lambda_k1, lambda_q2, lambda_k2, a_out_norm, b_out_norm, w_out,
           ffn2_norm, ffn2_w_in, ffn2_w_out):
    for l in range(ffn1_norm.shape[0]):
        x = _layer(x, l, ffn1_norm[l], ffn1_w_in[l], ffn1_w_out[l], mix_norm[l], w_in[l],
                   a_q_norm[l], a_k_norm[l], b_q_norm[l], b_k_norm[l],
                   lambda_q1[l], lambda_k1[l], lambda_q2[l], lambda_k2[l],
                   a_out_norm[l], b_out_norm[l], w_out[l],
                   ffn2_norm[l], ffn2_w_in[l], ffn2_w_out[l])
    return x
```

```python
import functools
import math

import numpy as np
import jax
import jax.numpy as jnp
from jax import lax
from jax.experimental import pallas as pl
from jax.experimental.pallas import tpu as pltpu

F32 = jnp.float32
BF16 = jnp.bfloat16

HEAD_DIM = 128
N_HEADS = 8
GROUP_WIDTH = N_HEADS * HEAD_DIM
B_SUB_DIM = 64
ROPE_THETA = 500000.0
ROPE_FRACTION = 4
PATTERNS = ((128, 1), (512, 4), (2048, 16))
HALF_WIN = 64
EPS = 1e-6
NEG = -1e30
LOG2E = math.log2(math.e)

Q_BLK = 128
K_BLK = Q_BLK + 2 * HALF_WIN

VMEM_LIMIT = 56 * 1024 * 1024


def _cparams(sem):
    return pltpu.CompilerParams(dimension_semantics=sem, vmem_limit_bytes=VMEM_LIMIT)


def _ffn_kernel(x_ref, g_ref, wg_ref, wu_ref, wo_ref, o_ref, h_ref):
    j = pl.program_id(1)

    @pl.when(j == 0)
    def _():
        x = x_ref[...]
        ms = jnp.mean(x * x, axis=-1, keepdims=True)
        h_ref[...] = (x * lax.rsqrt(ms + EPS) * g_ref[...]).astype(BF16)
        o_ref[...] = x

    h = h_ref[...]
    gate = jnp.dot(h, wg_ref[...], preferred_element_type=F32)
    up = jnp.dot(h, wu_ref[...], preferred_element_type=F32)
    act = (gate * jax.nn.sigmoid(gate) * up * 0.5).astype(BF16)
    o_ref[...] += jnp.dot(act, wo_ref[...], preferred_element_type=F32)


def _ffn(x2d, gain, w_in, w_out, *, tm=512, tf=512):
    T, D = x2d.shape
    d_ff = w_out.shape[0]
    nj = d_ff // tf
    return pl.pallas_call(
        _ffn_kernel,
        grid=(T // tm, nj),
        in_specs=[
            pl.BlockSpec((tm, D), lambda i, j: (i, 0)),
            pl.BlockSpec((1, D), lambda i, j: (0, 0)),
            pl.BlockSpec((D, tf), lambda i, j: (0, j)),
            pl.BlockSpec((D, tf), lambda i, j: (0, j + nj)),
            pl.BlockSpec((tf, D), lambda i, j: (j, 0)),
        ],
        out_specs=pl.BlockSpec((tm, D), lambda i, j: (i, 0)),
        out_shape=jax.ShapeDtypeStruct((T, D), F32),
        scratch_shapes=[pltpu.VMEM((tm, D), BF16)],
        compiler_params=_cparams(("parallel", "arbitrary")),
        name="ffn",
    )(x2d, gain.reshape(1, D), w_in, w_in, w_out)


def _rope_tables(seq, sub_dim):
    rd = sub_dim // ROPE_FRACTION
    half = rd // 2
    inv = ROPE_THETA ** (-np.arange(0, rd, 2, dtype=np.float64) / rd)
    ang = np.arange(seq, dtype=np.float64)[:, None] * inv[None, :]
    cos, sin = np.cos(ang), np.sin(ang)
    c = np.ones((seq, sub_dim))
    s = np.zeros((seq, sub_dim))
    c[:, :half] = cos
    c[:, half:rd] = cos
    s[:, :half] = -sin
    s[:, half:rd] = sin
    reps = HEAD_DIM // sub_dim
    return (np.tile(c, (1, reps)).astype(np.float32), np.tile(s, (1, reps)).astype(np.float32))


def _inproj_kernel(x_ref, g_ref, w_ref, cos_ref, sin_ref, gain_ref, o_ref, *, sub_dim, q_scale):
    x = x_ref[...]
    ms = jnp.mean(x * x, axis=-1, keepdims=True)
    h = (x * lax.rsqrt(ms + EPS) * g_ref[...]).astype(BF16)

    half = sub_dim // ROPE_FRACTION // 2
    lane = lax.broadcasted_iota(jnp.int32, (1, HEAD_DIM), 1)
    first_half = lane % sub_dim < half
    lo = lane < sub_dim
    cos = cos_ref[...]
    sin = sin_ref[...]
    gains = (gain_ref[0] * q_scale, gain_ref[1])
    pair = 2 * HEAD_DIM
    pairs_per_group = GROUP_WIDTH // pair

    def project(t):
        return jnp.dot(h, w_ref[:, t * pair:(t + 1) * pair], preferred_element_type=F32)

    def qk_epilogue(ph, gain):
        sq = ph * ph
        if sub_dim == HEAD_DIM:
            inv = lax.rsqrt(jnp.sum(sq, axis=-1, keepdims=True) * (1.0 / sub_dim) + EPS)
        else:
            ms_lo = jnp.sum(jnp.where(lo, sq, 0.0), axis=-1, keepdims=True) * (1.0 / sub_dim)
            ms_hi = jnp.sum(jnp.where(lo, 0.0, sq), axis=-1, keepdims=True) * (1.0 / sub_dim)
            inv = jnp.where(lo, lax.rsqrt(ms_lo + EPS), lax.rsqrt(ms_hi + EPS))
        y = ph * inv * gain
        rot = jnp.where(first_half, pltpu.roll(y, HEAD_DIM - half, 1), pltpu.roll(y, half, 1))
        return y * cos + rot * sin

    n_pairs = 3 * pairs_per_group
    p_next = project(0)
    for t in range(n_pairs):
        p, p_next = p_next, (project(t + 1) if t + 1 < n_pairs else None)
        grp, pr = divmod(t, pairs_per_group)
        for e in range(2):
            ph = p[:, e * HEAD_DIM:(e + 1) * HEAD_DIM]
            if grp < 2:
                ph = qk_epilogue(ph, gains[grp])
            o_ref[grp, 2 * pr + e] = ph.astype(o_ref.dtype)


def _inproj(x2d, mix_gain, w_in, col0, q_gain, k_gain, *, batch, seq, sub_dim, out_dtype, tm=512):
    T, D = x2d.shape
    spb = seq // tm
    cos, sin = _rope_tables(seq, sub_dim)
    reps = HEAD_DIM // sub_dim
    gains = jnp.stack([jnp.tile(q_gain, reps), jnp.tile(k_gain, reps)]).reshape(2, 1, HEAD_DIM)
    kern = functools.partial(_inproj_kernel, sub_dim=sub_dim, q_scale=LOG2E * sub_dim ** -0.5)
    return pl.pallas_call(
        kern,
        grid=(T // tm,),
        in_specs=[
            pl.BlockSpec((tm, D), lambda i: (i, 0)),
            pl.BlockSpec((1, D), lambda i: (0, 0)),
            pl.BlockSpec((D, 3 * GROUP_WIDTH), lambda i: (0, col0 // (3 * GROUP_WIDTH))),
            pl.BlockSpec((tm, HEAD_DIM), lambda i: (i % spb, 0)),
            pl.BlockSpec((tm, HEAD_DIM), lambda i: (i % spb, 0)),
            pl.BlockSpec((2, 1, HEAD_DIM), lambda i: (0, 0, 0)),
        ],
        out_specs=pl.BlockSpec((3, None, N_HEADS, tm, HEAD_DIM),
                               lambda i: (0, i // spb, 0, i % spb, 0)),
        out_shape=jax.ShapeDtypeStruct((3, batch, N_HEADS, seq, HEAD_DIM), out_dtype),
        compiler_params=_cparams(("parallel",)),
        name="inproj_%d" % sub_dim,
    )(x2d, mix_gain.reshape(1, D), w_in, jnp.asarray(cos), jnp.asarray(sin), gains)


def _band_bias():
    col_minus_row = np.arange(K_BLK)[None, :] - np.arange(Q_BLK)[:, None]
    return np.stack([np.where(np.abs(col_minus_row - lead) <= HALF_WIN, 0.0, NEG)
                     for lead in (0, HALF_WIN, 2 * HALF_WIN)]).astype(np.float32)


def _dilated_kernel(q_ref, k_ref, v_ref, bias_ref, g_ref, o_ref,
                    qs_ref, ks_ref, vs_ref, og_ref, lg_ref, *, seq):
    stage_rows = 256
    for g, (_, dil) in enumerate(PATTERNS):
        sub_len = seq // dil
        per_res = sub_len // stage_rows

        def stage(t, carry, g=g, dil=dil, sub_len=sub_len, per_res=per_res):
            r = t // per_res
            c0 = (t % per_res) * stage_rows
            dst = pl.ds(pl.multiple_of(r * sub_len + c0, stage_rows), stage_rows)
            if dil == 1:
                src = dst
            else:
                src = pl.ds(r + dil * c0, stage_rows, stride=dil)
            qs_ref[g, dst, :] = q_ref[src, :].astype(BF16)
            ks_ref[g, dst, :] = k_ref[src, :].astype(BF16)
            vs_ref[g, dst, :] = v_ref[src, :].astype(BF16)
            return carry

        lax.fori_loop(0, seq // stage_rows, stage, 0)

    unroll = 8
    for g, (_, dil) in enumerate(PATTERNS):
        sub_len = seq // dil
        nblk = sub_len // Q_BLK

        def body(it, carry, g=g, dil=dil, sub_len=sub_len, nblk=nblk):
            kvs, outs, scores = [], [], []
            for u in range(unroll):
                t = it * unroll + u
                r = t // nblk
                m0 = (t % nblk) * Q_BLK
                k0 = jnp.clip(m0 - HALF_WIN, 0, sub_len - K_BLK)
                base = r * sub_len
                q = qs_ref[g, pl.ds(pl.multiple_of(base + m0, Q_BLK), Q_BLK), :]
                kv = pl.ds(pl.multiple_of(base + k0, HALF_WIN), K_BLK)
                s = lax.dot_general(q, ks_ref[g, kv, :], (((1,), (1,)), ((), ())),
                                    preferred_element_type=F32)
                scores.append(s + bias_ref[(m0 - k0) // HALF_WIN])
                kvs.append(kv)
                if dil == 1:
                    outs.append(pl.ds(pl.multiple_of(m0, Q_BLK), Q_BLK))
                else:
                    outs.append(pl.ds(r + dil * m0, Q_BLK, stride=dil))
            s = jnp.concatenate(scores, axis=0)
            m = jnp.max(s, axis=-1, keepdims=True)
            p = jnp.exp2(s - m)
            l = jnp.sum(p, axis=-1, keepdims=True)
            p = p.astype(BF16)
            inv_l = 1.0 / l
            lse = jnp.broadcast_to(m + jnp.log2(l), (unroll * Q_BLK, HEAD_DIM))
            for u in range(unroll):
                blk = slice(u * Q_BLK, (u + 1) * Q_BLK)
                o = jnp.dot(p[blk], vs_ref[g, kvs[u], :], preferred_element_type=F32)
                og_ref[g, outs[u], :] = o * inv_l[blk]
                lg_ref[g, outs[u], :] = lse[blk]
            return carry

        lax.fori_loop(0, dil * nblk // unroll, body, 0)

    chunk = 256

    def comb(c, carry):
        rows = pl.ds(pl.multiple_of(c * chunk, chunk), chunk)
        l0, l1, l2 = lg_ref[0, rows, :], lg_ref[1, rows, :], lg_ref[2, rows, :]
        m = jnp.maximum(jnp.maximum(l0, l1), l2)
        w0, w1, w2 = jnp.exp2(l0 - m), jnp.exp2(l1 - m), jnp.exp2(l2 - m)
        o = (w0 * og_ref[0, rows, :] + w1 * og_ref[1, rows, :] + w2 * og_ref[2, rows, :]) / (w0 + w1 + w2)
        ms = jnp.mean(o * o, axis=-1, keepdims=True)
        o_ref[rows, :] = (o * lax.rsqrt(ms + EPS) * g_ref[...]).astype(o_ref.dtype)
        return carry

    lax.fori_loop(0, seq // chunk, comb, 0)


def _dilated(qkv, out_gain):
    _, batch, nh, seq, hd = qkv.shape
    npat = len(PATTERNS)
    spec = lambda which: pl.BlockSpec((None, None, None, seq, hd), lambda b, h: (which, b, h, 0, 0))
    return pl.pallas_call(
        functools.partial(_dilated_kernel, seq=seq),
        grid=(batch, nh),
        in_specs=[spec(0), spec(1), spec(2),
                  pl.BlockSpec((3, Q_BLK, K_BLK), lambda b, h: (0, 0, 0)),
                  pl.BlockSpec((1, hd), lambda b, h: (0, 0))],
        out_specs=pl.BlockSpec((None, seq, hd), lambda b, h: (b, 0, h)),
        out_shape=jax.ShapeDtypeStruct((batch, seq, nh * hd), BF16),
        scratch_shapes=[pltpu.VMEM((npat, seq, hd), BF16),
                        pltpu.VMEM((npat, seq, hd), BF16),
                        pltpu.VMEM((npat, seq, hd), BF16),
                        pltpu.VMEM((npat, seq, hd), F32),
                        pltpu.VMEM((npat, seq, hd), F32)],
        compiler_params=_cparams(("parallel", "parallel")),
        name="dilated",
    )(qkv, qkv, qkv, jnp.asarray(_band_bias()), out_gain.reshape(1, hd))


def _diff_kernel(lam_ref, q_ref, k_ref, v_ref, g_ref, o_ref, vaug_ref, *, out_scale, lambda_init, tq, kc):
    seq = k_ref.shape[0]
    vaug_ref[:, :HEAD_DIM] = v_ref[...]
    vaug_ref[:, HEAD_DIM:] = jnp.ones((seq, HEAD_DIM), BF16)

    lp = lam_ref[...]
    lam = (jnp.exp(jnp.sum(lp[0:1] * lp[1:2], axis=-1, keepdims=True))
           - jnp.exp(jnp.sum(lp[2:3] * lp[3:4], axis=-1, keepdims=True)) + lambda_init)
    gain = g_ref[...] * out_scale
    lane = lax.broadcasted_iota(jnp.int32, (1, HEAD_DIM), 1)
    nc = seq // kc

    def tile(i, carry):
        rows = pl.ds(pl.multiple_of(i * tq, tq), tq)
        q = q_ref[rows, :]
        zero = jnp.zeros_like(q)
        q_st = jnp.concatenate([jnp.where(lane < B_SUB_DIM, q, zero),
                                jnp.where(lane < B_SUB_DIM, zero, q)], axis=0)

        def scores(c):
            return lax.dot_general(q_st, k_ref[c * kc:(c + 1) * kc, :], (((1,), (1,)), ((), ())),
                                   preferred_element_type=F32)

        m = acc = None
        s_next = scores(0)
        for c in range(nc):
            s, s_next = s_next, (scores(c + 1) if c + 1 < nc else None)
            m_c = jnp.max(s, axis=-1, keepdims=True)
            m_new = m_c if m is None else jnp.maximum(m, m_c)
            pv = jnp.dot(jnp.exp2(s - m_new).astype(BF16), vaug_ref[c * kc:(c + 1) * kc, :],
                         preferred_element_type=F32)
            acc = pv if m is None else acc * jnp.exp2(m - m_new) + pv
            m = m_new
        o_st = acc[:, :HEAD_DIM] / acc[:, HEAD_DIM:]
        o = o_st[:tq] - lam * o_st[tq:]
        ms = jnp.mean(o * o, axis=-1, keepdims=True)
        o_ref[rows, :] = (o * lax.rsqrt(ms + EPS) * gain).astype(o_ref.dtype)
        return carry

    lax.fori_loop(0, seq // tq, tile, 0)


def _diff(qkv, lam_params, out_gain, *, lambda_init, tq=256, kc=1024):
    _, batch, nh, seq, hd = qkv.shape
    kern = functools.partial(_diff_kernel, out_scale=1.0 - lambda_init, lambda_init=lambda_init,
                             tq=tq, kc=kc)
    spec = lambda which: pl.BlockSpec((None, None, None, seq, hd), lambda b, h: (which, b, h, 0, 0))
    return pl.pallas_call(
        kern,
        grid=(batch, nh),
        in_specs=[pl.BlockSpec((4, B_SUB_DIM), lambda b, h: (0, 0)),
                  spec(0), spec(1), spec(2),
                  pl.BlockSpec((1, hd), lambda b, h: (0, 0))],
        out_specs=pl.BlockSpec((None, seq, hd), lambda b, h: (b, 0, h)),
        out_shape=jax.ShapeDtypeStruct((batch, seq, nh * hd), BF16),
        scratch_shapes=[pltpu.VMEM((seq, 2 * hd), BF16)],
        compiler_params=_cparams(("parallel", "parallel")),
        name="diff",
    )(lam_params, qkv, qkv, qkv, out_gain.reshape(1, hd))


def _outproj_kernel(x_ref, a_ref, b_ref, wa_ref, wb_ref, o_ref):
    o_ref[...] = (x_ref[...]
                  + jnp.dot(a_ref[...], wa_ref[...], preferred_element_type=F32)
                  + jnp.dot(b_ref[...], wb_ref[...], preferred_element_type=F32))


def _outproj(x2d, a2d, b2d, w_out, *, tm=512):
    T, D = x2d.shape
    W = a2d.shape[1]
    return pl.pallas_call(
        _outproj_kernel,
        grid=(T // tm,),
        in_specs=[
            pl.BlockSpec((tm, D), lambda i: (i, 0)),
            pl.BlockSpec((tm, W), lambda i: (i, 0)),
            pl.BlockSpec((tm, W), lambda i: (i, 0)),
            pl.BlockSpec((W, D), lambda i: (0, 0)),
            pl.BlockSpec((W, D), lambda i: (1, 0)),
        ],
        out_specs=pl.BlockSpec((tm, D), lambda i: (i, 0)),
        out_shape=jax.ShapeDtypeStruct((T, D), F32),
        compiler_params=_cparams(("parallel",)),
        name="outproj",
    )(x2d, a2d, b2d, w_out, w_out)


def _layer(x, layer_idx, ffn1_norm, ffn1_w_in, ffn1_w_out, mix_norm, w_in,
           a_q_norm, a_k_norm, b_q_norm, b_k_norm,
           lambda_q1, lambda_k1, lambda_q2, lambda_k2,
           a_out_norm, b_out_norm, w_out, ffn2_norm, ffn2_w_in, ffn2_w_out):
    batch, seq, d_model = x.shape
    x2d = x.reshape(batch * seq, d_model)
    lambda_init = 0.8 - 0.6 * math.exp(-0.3 * layer_idx)

    x1 = _ffn(x2d, ffn1_norm, ffn1_w_in.astype(BF16), ffn1_w_out.astype(BF16))

    w_in16 = w_in.astype(BF16)
    qkv_a = _inproj(x1, mix_norm, w_in16, 0, a_q_norm, a_k_norm,
                    batch=batch, seq=seq, sub_dim=HEAD_DIM, out_dtype=F32)
    qkv_b = _inproj(x1, mix_norm, w_in16, 3 * GROUP_WIDTH, b_q_norm, b_k_norm,
                    batch=batch, seq=seq, sub_dim=B_SUB_DIM, out_dtype=BF16)

    a_o = _dilated(qkv_a, a_out_norm)
    lam_params = jnp.stack([lambda_q1, lambda_k1, lambda_q2, lambda_k2])
    b_o = _diff(qkv_b, lam_params, b_out_norm, lambda_init=lambda_init)

    x2 = _outproj(x1, a_o.reshape(batch * seq, GROUP_WIDTH), b_o.reshape(batch * seq, GROUP_WIDTH),
                  w_out.astype(BF16))
    out = _ffn(x2, ffn2_norm, ffn2_w_in.astype(BF16), ffn2_w_out.astype(BF16))
    return out.reshape(batch, seq, d_model)


def kernel(x, ffn1_norm, ffn1_w_in, ffn1_w_out, mix_norm, w_in, a_q_norm, a_k_norm, b_q_norm, b_k_norm,
           lambda_q1, lambda_k1, lambda_q2, lambda_k2, a_out_norm, b_out_norm, w_out,
           ffn2_norm, ffn2_w_in, ffn2_w_out):
    for l in range(ffn1_norm.shape[0]):
        x = _layer(x, l, ffn1_norm[l], ffn1_w_in[l], ffn1_w_out[l], mix_norm[l], w_in[l],
                   a_q_norm[l], a_k_norm[l], b_q_norm[l], b_k_norm[l],
                   lambda_q1[l], lambda_k1[l], lambda_q2[l], lambda_k2[l],
                   a_out_norm[l], b_out_norm[l], w_out[l],
                   ffn2_norm[l], ffn2_w_in[l], ffn2_w_out[l])
    return x
```

```python
import functools
import math

import numpy as np
import jax
import jax.numpy as jnp
from jax import lax
from jax.experimental import pallas as pl
from jax.experimental.pallas import tpu as pltpu

F32 = jnp.float32
BF16 = jnp.bfloat16

HEAD_DIM = 128
N_HEADS = 8
GROUP_WIDTH = N_HEADS * HEAD_DIM
B_SUB_DIM = 64
ROPE_THETA = 500000.0
ROPE_FRACTION = 4
PATTERNS = ((128, 1), (512, 4), (2048, 16))
HALF_WIN = 64
EPS = 1e-6
NEG = -1e30
LOG2E = math.log2(math.e)

Q_BLK = 128
K_BLK = Q_BLK + 2 * HALF_WIN

VMEM_LIMIT = 60 * 1024 * 1024


def _cparams(sem):
    return pltpu.CompilerParams(dimension_semantics=sem, vmem_limit_bytes=VMEM_LIMIT)


def _ffn_kernel(x_ref, g_ref, wg_ref, wu_ref, wo_ref, o_ref, h_ref):
    j = pl.program_id(1)

    @pl.when(j == 0)
    def _():
        x = x_ref[...]
        ms = jnp.mean(x * x, axis=-1, keepdims=True)
        h_ref[...] = (x * lax.rsqrt(ms + EPS) * g_ref[...]).astype(BF16)
        o_ref[...] = x

    h = h_ref[...]
    gate = jnp.dot(h, wg_ref[...].astype(BF16), preferred_element_type=F32)
    up = jnp.dot(h, wu_ref[...].astype(BF16), preferred_element_type=F32)
    act = (gate * jax.nn.sigmoid(gate) * up * 0.5).astype(BF16)
    o_ref[...] += jnp.dot(act, wo_ref[...].astype(BF16), preferred_element_type=F32)


def _ffn(x2d, gain, w_in, w_out, *, tm=1024, tf=256):
    T, D = x2d.shape
    d_ff = w_out.shape[0]
    nj = d_ff // tf
    return pl.pallas_call(
        _ffn_kernel,
        grid=(T // tm, nj),
        in_specs=[
            pl.BlockSpec((tm, D), lambda i, j: (i, 0)),
            pl.BlockSpec((1, D), lambda i, j: (0, 0)),
            pl.BlockSpec((D, tf), lambda i, j: (0, j)),
            pl.BlockSpec((D, tf), lambda i, j: (0, j + nj)),
            pl.BlockSpec((tf, D), lambda i, j: (j, 0)),
        ],
        out_specs=pl.BlockSpec((tm, D), lambda i, j: (i, 0)),
        out_shape=jax.ShapeDtypeStruct((T, D), F32),
        scratch_shapes=[pltpu.VMEM((tm, D), BF16)],
        compiler_params=_cparams(("parallel", "arbitrary")),
        name="ffn",
    )(x2d, gain.reshape(1, D), w_in, w_in, w_out)


def _rope_tables(seq, sub_dim):
    rd = sub_dim // ROPE_FRACTION
    half = rd // 2
    inv = ROPE_THETA ** (-np.arange(0, rd, 2, dtype=np.float64) / rd)
    ang = np.arange(seq, dtype=np.float64)[:, None] * inv[None, :]
    cos, sin = np.cos(ang), np.sin(ang)
    c = np.ones((seq, sub_dim))
    s = np.zeros((seq, sub_dim))
    c[:, :half] = cos
    c[:, half:rd] = cos
    s[:, :half] = -sin
    s[:, half:rd] = sin
    reps = HEAD_DIM // sub_dim
    return (np.tile(c, (1, reps)).astype(np.float32), np.tile(s, (1, reps)).astype(np.float32))


def _inproj_kernel(x_ref, g_ref, w_ref, cos_ref, sin_ref, gain_ref, o_ref, *, sub_dim, q_scale):
    x = x_ref[...]
    ms = jnp.mean(x * x, axis=-1, keepdims=True)
    h = (x * lax.rsqrt(ms + EPS) * g_ref[...]).astype(BF16)

    half = sub_dim // ROPE_FRACTION // 2
    lane = lax.broadcasted_iota(jnp.int32, (1, HEAD_DIM), 1)
    first_half = lane % sub_dim < half
    lo = lane < sub_dim
    cos = cos_ref[...]
    sin = sin_ref[...]
    gains = (gain_ref[0] * q_scale, gain_ref[1])
    pair = 2 * HEAD_DIM
    pairs_per_group = GROUP_WIDTH // pair

    def project(t):
        return jnp.dot(h, w_ref[:, t * pair:(t + 1) * pair], preferred_element_type=F32)

    def qk_epilogue(ph, gain):
        sq = ph * ph
        if sub_dim == HEAD_DIM:
            inv = lax.rsqrt(jnp.sum(sq, axis=-1, keepdims=True) * (1.0 / sub_dim) + EPS)
        else:
            ms_lo = jnp.sum(jnp.where(lo, sq, 0.0), axis=-1, keepdims=True) * (1.0 / sub_dim)
            ms_hi = jnp.sum(jnp.where(lo, 0.0, sq), axis=-1, keepdims=True) * (1.0 / sub_dim)
            inv = jnp.where(lo, lax.rsqrt(ms_lo + EPS), lax.rsqrt(ms_hi + EPS))
        y = ph * inv * gain
        rot = jnp.where(first_half, pltpu.roll(y, HEAD_DIM - half, 1), pltpu.roll(y, half, 1))
        return y * cos + rot * sin

    n_pairs = 3 * pairs_per_group
    p_next = project(0)
    for t in range(n_pairs):
        p, p_next = p_next, (project(t + 1) if t + 1 < n_pairs else None)
        grp, pr = divmod(t, pairs_per_group)
        for e in range(2):
            ph = p[:, e * HEAD_DIM:(e + 1) * HEAD_DIM]
            if grp < 2:
                ph = qk_epilogue(ph, gains[grp])
            o_ref[grp, 2 * pr + e] = ph.astype(o_ref.dtype)


def _inproj(x2d, mix_gain, w_in, col0, q_gain, k_gain, *, batch, seq, sub_dim, out_dtype, tm=512):
    T, D = x2d.shape
    spb = seq // tm
    cos, sin = _rope_tables(seq, sub_dim)
    reps = HEAD_DIM // sub_dim
    gains = jnp.stack([jnp.tile(q_gain, reps), jnp.tile(k_gain, reps)]).reshape(2, 1, HEAD_DIM)
    kern = functools.partial(_inproj_kernel, sub_dim=sub_dim, q_scale=LOG2E * sub_dim ** -0.5)
    return pl.pallas_call(
        kern,
        grid=(T // tm,),
        in_specs=[
            pl.BlockSpec((tm, D), lambda i: (i, 0)),
            pl.BlockSpec((1, D), lambda i: (0, 0)),
            pl.BlockSpec((D, 3 * GROUP_WIDTH), lambda i: (0, col0 // (3 * GROUP_WIDTH))),
            pl.BlockSpec((tm, HEAD_DIM), lambda i: (i % spb, 0)),
            pl.BlockSpec((tm, HEAD_DIM), lambda i: (i % spb, 0)),
            pl.BlockSpec((2, 1, HEAD_DIM), lambda i: (0, 0, 0)),
        ],
        out_specs=pl.BlockSpec((3, None, N_HEADS, tm, HEAD_DIM),
                               lambda i: (0, i // spb, 0, i % spb, 0)),
        out_shape=jax.ShapeDtypeStruct((3, batch, N_HEADS, seq, HEAD_DIM), out_dtype),
        compiler_params=_cparams(("parallel",)),
        name="inproj_%d" % sub_dim,
    )(x2d, mix_gain.reshape(1, D), w_in, jnp.asarray(cos), jnp.asarray(sin), gains)


def _band_bias():
    col_minus_row = np.arange(K_BLK)[None, :] - np.arange(Q_BLK)[:, None]
    return np.stack([np.where(np.abs(col_minus_row - lead) <= HALF_WIN, 0.0, NEG)
                     for lead in (0, HALF_WIN, 2 * HALF_WIN)]).astype(np.float32)


def _dilated_kernel(q_ref, k_ref, v_ref, bias_ref, g_ref, o_ref,
                    qs_ref, ks_ref, vs_ref, og_ref, lg_ref, *, seq):
    stage_rows = 256
    for g, (_, dil) in enumerate(PATTERNS):
        sub_len = seq // dil
        per_res = sub_len // stage_rows

        def stage(t, carry, g=g, dil=dil, sub_len=sub_len, per_res=per_res):
            r = t // per_res
            c0 = (t % per_res) * stage_rows
            dst = pl.ds(pl.multiple_of(r * sub_len + c0, stage_rows), stage_rows)
            if dil == 1:
                src = dst
            else:
                src = pl.ds(r + dil * c0, stage_rows, stride=dil)
            qs_ref[g, dst, :] = q_ref[src, :].astype(BF16)
            ks_ref[g, dst, :] = k_ref[src, :].astype(BF16)
            vs_ref[g, dst, :] = v_ref[src, :].astype(BF16)
            return carry

        lax.fori_loop(0, seq // stage_rows, stage, 0)

    unroll = 8
    for g, (_, dil) in enumerate(PATTERNS):
        sub_len = seq // dil
        nblk = sub_len // Q_BLK

        def body(it, carry, g=g, dil=dil, sub_len=sub_len, nblk=nblk):
            kvs, outs, scores = [], [], []
            for u in range(unroll):
                t = it * unroll + u
                r = t // nblk
                m0 = (t % nblk) * Q_BLK
                k0 = jnp.clip(m0 - HALF_WIN, 0, sub_len - K_BLK)
                base = r * sub_len
                q = qs_ref[g, pl.ds(pl.multiple_of(base + m0, Q_BLK), Q_BLK), :]
                kv = pl.ds(pl.multiple_of(base + k0, HALF_WIN), K_BLK)
                s = lax.dot_general(q, ks_ref[g, kv, :], (((1,), (1,)), ((), ())),
                                    preferred_element_type=F32)
                scores.append(s + bias_ref[(m0 - k0) // HALF_WIN])
                kvs.append(kv)
                if dil == 1:
                    outs.append(pl.ds(pl.multiple_of(m0, Q_BLK), Q_BLK))
                else:
                    outs.append(pl.ds(r + dil * m0, Q_BLK, stride=dil))
            s = jnp.concatenate(scores, axis=0)
            m = jnp.max(s, axis=-1, keepdims=True)
            p = jnp.exp2(s - m)
            l = jnp.sum(p, axis=-1, keepdims=True)
            p = p.astype(BF16)
            inv_l = 1.0 / l
            lse = jnp.broadcast_to(m + jnp.log2(l), (unroll * Q_BLK, HEAD_DIM))
            for u in range(unroll):
                blk = slice(u * Q_BLK, (u + 1) * Q_BLK)
                o = jnp.dot(p[blk], vs_ref[g, kvs[u], :], preferred_element_type=F32)
                og_ref[g, outs[u], :] = o * inv_l[blk]
                lg_ref[g, outs[u], :] = lse[blk]
            return carry

        lax.fori_loop(0, dil * nblk // unroll, body, 0)

    chunk = 256

    def comb(c, carry):
        rows = pl.ds(pl.multiple_of(c * chunk, chunk), chunk)
        l0, l1, l2 = lg_ref[0, rows, :], lg_ref[1, rows, :], lg_ref[2, rows, :]
        m = jnp.maximum(jnp.maximum(l0, l1), l2)
        w0, w1, w2 = jnp.exp2(l0 - m), jnp.exp2(l1 - m), jnp.exp2(l2 - m)
        o = (w0 * og_ref[0, rows, :] + w1 * og_ref[1, rows, :] + w2 * og_ref[2, rows, :]) / (w0 + w1 + w2)
        ms = jnp.mean(o * o, axis=-1, keepdims=True)
        o_ref[rows, :] = (o * lax.rsqrt(ms + EPS) * g_ref[...]).astype(o_ref.dtype)
        return carry

    lax.fori_loop(0, seq // chunk, comb, 0)


def _dilated(qkv, out_gain):
    _, batch, nh, seq, hd = qkv.shape
    npat = len(PATTERNS)
    spec = lambda which: pl.BlockSpec((None, None, None, seq, hd), lambda b, h: (which, b, h, 0, 0))
    return pl.pallas_call(
        functools.partial(_dilated_kernel, seq=seq),
        grid=(batch, nh),
        in_specs=[spec(0), spec(1), spec(2),
                  pl.BlockSpec((3, Q_BLK, K_BLK), lambda b, h: (0, 0, 0)),
                  pl.BlockSpec((1, hd), lambda b, h: (0, 0))],
        out_specs=pl.BlockSpec((None, seq, hd), lambda b, h: (b, 0, h)),
        out_shape=jax.ShapeDtypeStruct((batch, seq, nh * hd), BF16),
        scratch_shapes=[pltpu.VMEM((npat, seq, hd), BF16),
                        pltpu.VMEM((npat, seq, hd), BF16),
                        pltpu.VMEM((npat, seq, hd), BF16),
                        pltpu.VMEM((npat, seq, hd), F32),
                        pltpu.VMEM((npat, seq, hd), F32)],
        compiler_params=_cparams(("parallel", "parallel")),
        name="dilated",
    )(qkv, qkv, qkv, jnp.asarray(_band_bias()), out_gain.reshape(1, hd))


def _diff_kernel(lam_ref, q_ref, k_ref, v_ref, g_ref, o_ref, vaug_ref, *, out_scale, lambda_init, tq, kc):
    seq = k_ref.shape[0]
    vaug_ref[:, :HEAD_DIM] = v_ref[...]
    vaug_ref[:, HEAD_DIM:] = jnp.ones((seq, HEAD_DIM), BF16)

    lp = lam_ref[...]
    lam = (jnp.exp(jnp.sum(lp[0:1] * lp[1:2], axis=-1, keepdims=True))
           - jnp.exp(jnp.sum(lp[2:3] * lp[3:4], axis=-1, keepdims=True)) + lambda_init)
    gain = g_ref[...] * out_scale
    lane = lax.broadcasted_iota(jnp.int32, (1, HEAD_DIM), 1)
    nc = seq // kc

    def tile(i, carry):
        rows = pl.ds(pl.multiple_of(i * tq, tq), tq)
        q = q_ref[rows, :]
        zero = jnp.zeros_like(q)
        q_st = jnp.concatenate([jnp.where(lane < B_SUB_DIM, q, zero),
                                jnp.where(lane < B_SUB_DIM, zero, q)], axis=0)

        def scores(c):
            return lax.dot_general(q_st, k_ref[c * kc:(c + 1) * kc, :], (((1,), (1,)), ((), ())),
                                   preferred_element_type=F32)

        m = acc = None
        s_next = scores(0)
        for c in range(nc):
            s, s_next = s_next, (scores(c + 1) if c + 1 < nc else None)
            m_c = jnp.max(s, axis=-1, keepdims=True)
            m_new = m_c if m is None else jnp.maximum(m, m_c)
            pv = jnp.dot(jnp.exp2(s - m_new).astype(BF16), vaug_ref[c * kc:(c + 1) * kc, :],
                         preferred_element_type=F32)
            acc = pv if m is None else acc * jnp.exp2(m - m_new) + pv
            m = m_new
        o_st = acc[:, :HEAD_DIM] / acc[:, HEAD_DIM:]
        o = o_st[:tq] - lam * o_st[tq:]
        ms = jnp.mean(o * o, axis=-1, keepdims=True)
        o_ref[rows, :] = (o * lax.rsqrt(ms + EPS) * gain).astype(o_ref.dtype)
        return carry

    lax.fori_loop(0, seq // tq, tile, 0)


def _diff(qkv, lam_params, out_gain, *, lambda_init, tq=256, kc=1024):
    _, batch, nh, seq, hd = qkv.shape
    kern = functools.partial(_diff_kernel, out_scale=1.0 - lambda_init, lambda_init=lambda_init,
                             tq=tq, kc=kc)
    spec = lambda which: pl.BlockSpec((None, None, None, seq, hd), lambda b, h: (which, b, h, 0, 0))
    return pl.pallas_call(
        kern,
        grid=(batch, nh),
        in_specs=[pl.BlockSpec((4, B_SUB_DIM), lambda b, h: (0, 0)),
                  spec(0), spec(1), spec(2),
                  pl.BlockSpec((1, hd), lambda b, h: (0, 0))],
        out_specs=pl.BlockSpec((None, seq, hd), lambda b, h: (b, 0, h)),
        out_shape=jax.ShapeDtypeStruct((batch, seq, nh * hd), BF16),
        scratch_shapes=[pltpu.VMEM((seq, 2 * hd), BF16)],
        compiler_params=_cparams(("parallel", "parallel")),
        name="diff",
    )(lam_params, qkv, qkv, qkv, out_gain.reshape(1, hd))


def _outproj_kernel(x_ref, a_ref, b_ref, wa_ref, wb_ref, o_ref):
    o_ref[...] = (x_ref[...]
                  + jnp.dot(a_ref[...], wa_ref[...], preferred_element_type=F32)
                  + jnp.dot(b_ref[...], wb_ref[...], preferred_element_type=F32))


def _outproj(x2d, a2d, b2d, w_out, *, tm=512):
    T, D = x2d.shape
    W = a2d.shape[1]
    return pl.pallas_call(
        _outproj_kernel,
        grid=(T // tm,),
        in_specs=[
            pl.BlockSpec((tm, D), lambda i: (i, 0)),
            pl.BlockSpec((tm, W), lambda i: (i, 0)),
            pl.BlockSpec((tm, W), lambda i: (i, 0)),
            pl.BlockSpec((W, D), lambda i: (0, 0)),
            pl.BlockSpec((W, D), lambda i: (1, 0)),
        ],
        out_specs=pl.BlockSpec((tm, D), lambda i: (i, 0)),
        out_shape=jax.ShapeDtypeStruct((T, D), F32),
        compiler_params=_cparams(("parallel",)),
        name="outproj",
    )(x2d, a2d, b2d, w_out, w_out)


def _layer(x, layer_idx, ffn1_norm, ffn1_w_in, ffn1_w_out, mix_norm, w_in,
           a_q_norm, a_k_norm, b_q_norm, b_k_norm,
           lambda_q1, lambda_k1, lambda_q2, lambda_k2,
           a_out_norm, b_out_norm, w_out, ffn2_norm, ffn2_w_in, ffn2_w_out):
    batch, seq, d_model = x.shape
    x2d = x.reshape(batch * seq, d_model)
    lambda_init = 0.8 - 0.6 * math.exp(-0.3 * layer_idx)

    x1 = _ffn(x2d, ffn1_norm, ffn1_w_in, ffn1_w_out)

    w_in16 = w_in.astype(BF16)
    qkv_a = _inproj(x1, mix_norm, w_in16, 0, a_q_norm, a_k_norm,
                    batch=batch, seq=seq, sub_dim=HEAD_DIM, out_dtype=F32)
    qkv_b = _inproj(x1, mix_norm, w_in16, 3 * GROUP_WIDTH, b_q_norm, b_k_norm,
                    batch=batch, seq=seq, sub_dim=B_SUB_DIM, out_dtype=BF16)

    a_o = _dilated(qkv_a, a_out_norm)
    lam_params = jnp.stack([lambda_q1, lambda_k1, lambda_q2, lambda_k2])
    b_o = _diff(qkv_b, lam_params, b_out_norm, lambda_init=lambda_init)

    x2 = _outproj(x1, a_o.reshape(batch * seq, GROUP_WIDTH), b_o.reshape(batch * seq, GROUP_WIDTH),
                  w_out.astype(BF16))
    out = _ffn(x2, ffn2_norm, ffn2_w_in, ffn2_w_out)
    return out.reshape(batch, seq, d_model)


def kernel(x, ffn1_norm, ffn1_w_in, ffn1_w_out, mix_norm, w_in, a_q_norm, a_k_norm, b_q_norm, b_k_norm,
           lambda_q1, lambda_k1, lambda_q2, lambda_k2, a_out_norm, b_out_norm, w_out,
           ffn2_norm, ffn2_w_in, ffn2_w_out):
    for l in range(ffn1_norm.shape[0]):
        x = _layer(x, l, ffn1_norm[l], ffn1_w_in[l], ffn1_w_out[l], mix_norm[l], w_in[l],
                   a_q_norm[l], a_k_norm[l], b_q_norm[l], b_k_norm[l],
                   lambda_q1[l], lambda_k1[l], lambda_q2[l], lambda_k2[l],
                   a_out_norm[l], b_out_norm[l], w_out[l],
                   ffn2_norm[l], ffn2_w_in[l], ffn2_w_out[l])
    return x
```

```python
import functools
import math

import numpy as np
import jax
import jax.numpy as jnp
from jax import lax
from jax.experimental import pallas as pl
from jax.experimental.pallas import tpu as pltpu

F32 = jnp.float32
BF16 = jnp.bfloat16

HEAD_DIM = 128
N_HEADS = 8
GROUP_WIDTH = N_HEADS * HEAD_DIM
B_SUB_DIM = 64
ROPE_THETA = 500000.0
ROPE_FRACTION = 4
PATTERNS = ((128, 1), (512, 4), (2048, 16))
DIL_STEP = 4
HALF_WIN = 64
EPS = 1e-6
NEG = -1e30
LOG2E = math.log2(math.e)

Q_BLK = 128
K_BLK = Q_BLK + 2 * HALF_WIN

VMEM_LIMIT = 60 * 1024 * 1024


def _cparams(sem):
    return pltpu.CompilerParams(dimension_semantics=sem, vmem_limit_bytes=VMEM_LIMIT)


def _ffn_kernel(x_ref, g_ref, wg_ref, wu_ref, wo_ref, o_ref, h_ref):
    j = pl.program_id(1)

    @pl.when(j == 0)
    def _():
        x = x_ref[...]
        ms = jnp.mean(x * x, axis=-1, keepdims=True)
        h_ref[...] = (x * lax.rsqrt(ms + EPS) * g_ref[...]).astype(BF16)
        o_ref[...] = x

    h = h_ref[...]
    gate = jnp.dot(h, wg_ref[...].astype(BF16), preferred_element_type=F32)
    up = jnp.dot(h, wu_ref[...].astype(BF16), preferred_element_type=F32)
    act = (gate * jax.nn.sigmoid(gate) * up * 0.5).astype(BF16)
    o_ref[...] += jnp.dot(act, wo_ref[...].astype(BF16), preferred_element_type=F32)


def _ffn(x2d, gain, w_in, w_out, *, tm=1024, tf=256):
    T, D = x2d.shape
    d_ff = w_out.shape[0]
    nj = d_ff // tf
    return pl.pallas_call(
        _ffn_kernel,
        grid=(T // tm, nj),
        in_specs=[
            pl.BlockSpec((tm, D), lambda i, j: (i, 0)),
            pl.BlockSpec((1, D), lambda i, j: (0, 0)),
            pl.BlockSpec((D, tf), lambda i, j: (0, j)),
            pl.BlockSpec((D, tf), lambda i, j: (0, j + nj)),
            pl.BlockSpec((tf, D), lambda i, j: (j, 0)),
        ],
        out_specs=pl.BlockSpec((tm, D), lambda i, j: (i, 0)),
        out_shape=jax.ShapeDtypeStruct((T, D), F32),
        scratch_shapes=[pltpu.VMEM((tm, D), BF16)],
        compiler_params=_cparams(("parallel", "arbitrary")),
        name="ffn",
    )(x2d, gain.reshape(1, D), w_in, w_in, w_out)


def _rope_tables(seq, sub_dim):
    rd = sub_dim // ROPE_FRACTION
    half = rd // 2
    inv = ROPE_THETA ** (-np.arange(0, rd, 2, dtype=np.float64) / rd)
    ang = np.arange(seq, dtype=np.float64)[:, None] * inv[None, :]
    cos, sin = np.cos(ang), np.sin(ang)
    c = np.ones((seq, sub_dim))
    s = np.zeros((seq, sub_dim))
    c[:, :half] = cos
    c[:, half:rd] = cos
    s[:, :half] = -sin
    s[:, half:rd] = sin
    reps = HEAD_DIM // sub_dim
    return (np.tile(c, (1, reps)).astype(np.float32), np.tile(s, (1, reps)).astype(np.float32))


def _inproj_kernel(x_ref, g_ref, w_ref, cos_ref, sin_ref, gain_ref, o_ref, *, sub_dim, q_scale):
    x = x_ref[...]
    ms = jnp.mean(x * x, axis=-1, keepdims=True)
    h = (x * lax.rsqrt(ms + EPS) * g_ref[...]).astype(BF16)

    half = sub_dim // ROPE_FRACTION // 2
    lane = lax.broadcasted_iota(jnp.int32, (1, HEAD_DIM), 1)
    first_half = lane % sub_dim < half
    lo = lane < sub_dim
    cos = cos_ref[...]
    sin = sin_ref[...]
    gains = (gain_ref[0] * q_scale, gain_ref[1])
    pair = 2 * HEAD_DIM
    pairs_per_group = GROUP_WIDTH // pair

    def project(t):
        return jnp.dot(h, w_ref[:, t * pair:(t + 1) * pair], preferred_element_type=F32)

    def qk_epilogue(ph, gain):
        sq = ph * ph
        if sub_dim == HEAD_DIM:
            inv = lax.rsqrt(jnp.sum(sq, axis=-1, keepdims=True) * (1.0 / sub_dim) + EPS)
        else:
            ms_lo = jnp.sum(jnp.where(lo, sq, 0.0), axis=-1, keepdims=True) * (1.0 / sub_dim)
            ms_hi = jnp.sum(jnp.where(lo, 0.0, sq), axis=-1, keepdims=True) * (1.0 / sub_dim)
            inv = jnp.where(lo, lax.rsqrt(ms_lo + EPS), lax.rsqrt(ms_hi + EPS))
        y = ph * inv * gain
        rot = jnp.where(first_half, pltpu.roll(y, HEAD_DIM - half, 1), pltpu.roll(y, half, 1))
        return y * cos + rot * sin

    n_pairs = 3 * pairs_per_group
    p_next = project(0)
    for t in range(n_pairs):
        p, p_next = p_next, (project(t + 1) if t + 1 < n_pairs else None)
        grp, pr = divmod(t, pairs_per_group)
        for e in range(2):
            ph = p[:, e * HEAD_DIM:(e + 1) * HEAD_DIM]
            if grp < 2:
                ph = qk_epilogue(ph, gains[grp])
            o_ref[grp, 2 * pr + e] = ph.astype(o_ref.dtype)


def _inproj(x2d, mix_gain, w_in, col0, q_gain, k_gain, *, batch, seq, sub_dim, out_dtype, tm=512):
    T, D = x2d.shape
    spb = seq // tm
    cos, sin = _rope_tables(seq, sub_dim)
    reps = HEAD_DIM // sub_dim
    gains = jnp.stack([jnp.tile(q_gain, reps), jnp.tile(k_gain, reps)]).reshape(2, 1, HEAD_DIM)
    kern = functools.partial(_inproj_kernel, sub_dim=sub_dim, q_scale=LOG2E * sub_dim ** -0.5)
    return pl.pallas_call(
        kern,
        grid=(T // tm,),
        in_specs=[
            pl.BlockSpec((tm, D), lambda i: (i, 0)),
            pl.BlockSpec((1, D), lambda i: (0, 0)),
            pl.BlockSpec((D, 3 * GROUP_WIDTH), lambda i: (0, col0 // (3 * GROUP_WIDTH))),
            pl.BlockSpec((tm, HEAD_DIM), lambda i: (i % spb, 0)),
            pl.BlockSpec((tm, HEAD_DIM), lambda i: (i % spb, 0)),
            pl.BlockSpec((2, 1, HEAD_DIM), lambda i: (0, 0, 0)),
        ],
        out_specs=pl.BlockSpec((3, None, N_HEADS, tm, HEAD_DIM),
                               lambda i: (0, i // spb, 0, i % spb, 0)),
        out_shape=jax.ShapeDtypeStruct((3, batch, N_HEADS, seq, HEAD_DIM), out_dtype),
        compiler_params=_cparams(("parallel",)),
        name="inproj_%d" % sub_dim,
    )(x2d, mix_gain.reshape(1, D), w_in, jnp.asarray(cos), jnp.asarray(sin), gains)


def _band_bias():
    col_minus_row = np.arange(K_BLK)[None, :] - np.arange(Q_BLK)[:, None]
    return np.stack([np.where(np.abs(col_minus_row - lead) <= HALF_WIN, 0.0, NEG)
                     for lead in (0, HALF_WIN, 2 * HALF_WIN)]).astype(np.float32)


def _dilated_kernel(q_ref, k_ref, v_ref, bias_ref, g_ref, o_ref,
                    qs_ref, ks_ref, vs_ref, mid_ref, og_ref, lg_ref, *, seq):
    stage_rows = 256
    srcs, dsts = (q_ref, k_ref, v_ref), (qs_ref, ks_ref, vs_ref)
    (_, dil0), (_, dil1), (_, dil2) = PATTERNS
    assert dil0 == 1 and dil1 == DIL_STEP and dil2 == DIL_STEP * dil1 and seq // dil2 == stage_rows
    sub1 = seq // dil1

    def stage0(t, carry):
        rows = pl.ds(pl.multiple_of(t * stage_rows, stage_rows), stage_rows)
        for src, dst in zip(srcs, dsts):
            dst[0, rows, :] = src[rows, :].astype(BF16)
        return carry

    def stage1(t, carry):
        per_res = sub1 // stage_rows
        r = t // per_res
        c0 = (t % per_res) * stage_rows
        rows = pl.ds(pl.multiple_of(r * sub1 + c0, stage_rows), stage_rows)
        for a, (src, dst) in enumerate(zip(srcs, dsts)):
            x = src[pl.ds(r + dil1 * c0, stage_rows, stride=dil1), :]
            mid_ref[a, rows, :] = x
            dst[1, rows, :] = x.astype(BF16)
        return carry

    def stage2(t, carry):
        r1 = t // DIL_STEP
        rr = t % DIL_STEP
        rows = pl.ds(pl.multiple_of((dil1 * rr + r1) * stage_rows, stage_rows), stage_rows)
        for a, dst in enumerate(dsts):
            dst[2, rows, :] = mid_ref[a, pl.ds(r1 * sub1 + rr, stage_rows, stride=DIL_STEP), :].astype(BF16)
        return carry

    for stage in (stage0, stage1, stage2):
        lax.fori_loop(0, seq // stage_rows, stage, 0)

    unroll = 8
    for g, (_, dil) in enumerate(PATTERNS):
        sub_len = seq // dil
        nblk = sub_len // Q_BLK

        def body(it, carry, g=g, dil=dil, sub_len=sub_len, nblk=nblk):
            kvs, outs, scores = [], [], []
            for u in range(unroll):
                t = it * unroll + u
                r = t // nblk
                m0 = (t % nblk) * Q_BLK
                k0 = jnp.clip(m0 - HALF_WIN, 0, sub_len - K_BLK)
                base = r * sub_len
                q = qs_ref[g, pl.ds(pl.multiple_of(base + m0, Q_BLK), Q_BLK), :]
                kv = pl.ds(pl.multiple_of(base + k0, HALF_WIN), K_BLK)
                s = lax.dot_general(q, ks_ref[g, kv, :], (((1,), (1,)), ((), ())),
                                    preferred_element_type=F32)
                scores.append(s + bias_ref[(m0 - k0) // HALF_WIN])
                kvs.append(kv)
                if dil == 1:
                    outs.append(pl.ds(pl.multiple_of(m0, Q_BLK), Q_BLK))
                else:
                    outs.append(pl.ds(r + dil * m0, Q_BLK, stride=dil))
            s = jnp.concatenate(scores, axis=0)
            m = jnp.max(s, axis=-1, keepdims=True)
            p = jnp.exp2(s - m)
            l = jnp.sum(p, axis=-1, keepdims=True)
            p = p.astype(BF16)
            inv_l = 1.0 / l
            lse = jnp.broadcast_to(m + jnp.log2(l), (unroll * Q_BLK, HEAD_DIM))
            for u in range(unroll):
                blk = slice(u * Q_BLK, (u + 1) * Q_BLK)
                o = jnp.dot(p[blk], vs_ref[g, kvs[u], :], preferred_element_type=F32)
                og_ref[g, outs[u], :] = o * inv_l[blk]
                lg_ref[g, outs[u], :] = lse[blk]
            return carry

        lax.fori_loop(0, dil * nblk // unroll, body, 0)

    chunk = 256

    def comb(c, carry):
        rows = pl.ds(pl.multiple_of(c * chunk, chunk), chunk)
        l0, l1, l2 = lg_ref[0, rows, :], lg_ref[1, rows, :], lg_ref[2, rows, :]
        m = jnp.maximum(jnp.maximum(l0, l1), l2)
        w0, w1, w2 = jnp.exp2(l0 - m), jnp.exp2(l1 - m), jnp.exp2(l2 - m)
        o = (w0 * og_ref[0, rows, :] + w1 * og_ref[1, rows, :] + w2 * og_ref[2, rows, :]) / (w0 + w1 + w2)
        ms = jnp.mean(o * o, axis=-1, keepdims=True)
        o_ref[rows, :] = (o * lax.rsqrt(ms + EPS) * g_ref[...]).astype(o_ref.dtype)
        return carry

    lax.fori_loop(0, seq // chunk, comb, 0)


def _dilated(qkv, out_gain):
    _, batch, nh, seq, hd = qkv.shape
    npat = len(PATTERNS)
    spec = lambda which: pl.BlockSpec((None, None, None, seq, hd), lambda b, h: (which, b, h, 0, 0))
    return pl.pallas_call(
        functools.partial(_dilated_kernel, seq=seq),
        grid=(batch, nh),
        in_specs=[spec(0), spec(1), spec(2),
                  pl.BlockSpec((3, Q_BLK, K_BLK), lambda b, h: (0, 0, 0)),
                  pl.BlockSpec((1, hd), lambda b, h: (0, 0))],
        out_specs=pl.BlockSpec((None, seq, hd), lambda b, h: (b, 0, h)),
        out_shape=jax.ShapeDtypeStruct((batch, seq, nh * hd), BF16),
        scratch_shapes=[pltpu.VMEM((npat, seq, hd), BF16),
                        pltpu.VMEM((npat, seq, hd), BF16),
                        pltpu.VMEM((npat, seq, hd), BF16),
                        pltpu.VMEM((3, seq, hd), F32),
                        pltpu.VMEM((npat, seq, hd), F32),
                        pltpu.VMEM((npat, seq, hd), F32)],
        compiler_params=_cparams(("parallel", "parallel")),
        name="dilated",
    )(qkv, qkv, qkv, jnp.asarray(_band_bias()), out_gain.reshape(1, hd))


def _diff_kernel(lam_ref, q_ref, k_ref, v_ref, g_ref, o_ref, vaug_ref, s0_ref, *,
                 out_scale, lambda_init, tq, kc):
    seq = k_ref.shape[0]
    vaug_ref[:, :HEAD_DIM] = v_ref[...]
    vaug_ref[:, HEAD_DIM:] = jnp.ones((seq, HEAD_DIM), BF16)

    lp = lam_ref[...]
    lam = (jnp.exp(jnp.sum(lp[0:1] * lp[1:2], axis=-1, keepdims=True))
           - jnp.exp(jnp.sum(lp[2:3] * lp[3:4], axis=-1, keepdims=True)) + lambda_init)
    gain = g_ref[...] * out_scale
    lane = lax.broadcasted_iota(jnp.int32, (1, HEAD_DIM), 1)
    nc = seq // kc
    n_tiles = seq // tq

    def stacked_q(i):
        q = q_ref[pl.ds(pl.multiple_of(i * tq, tq), tq), :]
        zero = jnp.zeros_like(q)
        return jnp.concatenate([jnp.where(lane < B_SUB_DIM, q, zero),
                                jnp.where(lane < B_SUB_DIM, zero, q)], axis=0)

    def scores(q_st, c):
        return lax.dot_general(q_st, k_ref[c * kc:(c + 1) * kc, :], (((1,), (1,)), ((), ())),
                               preferred_element_type=F32)

    s0_ref[...] = scores(stacked_q(0), 0)

    def tile(i, carry):
        rows = pl.ds(pl.multiple_of(i * tq, tq), tq)
        q_st = stacked_q(i)
        m = acc = None
        s_next = s0_ref[...]
        for c in range(nc):
            s = s_next
            if c + 1 < nc:
                s_next = scores(q_st, c + 1)
            else:
                s0_ref[...] = scores(stacked_q(jnp.minimum(i + 1, n_tiles - 1)), 0)
            m_c = jnp.max(s, axis=-1, keepdims=True)
            m_new = m_c if m is None else jnp.maximum(m, m_c)
            pv = jnp.dot(jnp.exp2(s - m_new).astype(BF16), vaug_ref[c * kc:(c + 1) * kc, :],
                         preferred_element_type=F32)
            acc = pv if m is None else acc * jnp.exp2(m - m_new) + pv
            m = m_new
        o_st = acc[:, :HEAD_DIM] / acc[:, HEAD_DIM:]
        o = o_st[:tq] - lam * o_st[tq:]
        ms = jnp.mean(o * o, axis=-1, keepdims=True)
        o_ref[rows, :] = (o * lax.rsqrt(ms + EPS) * gain).astype(o_ref.dtype)
        return carry

    lax.fori_loop(0, n_tiles, tile, 0)


def _diff(qkv, lam_params, out_gain, *, lambda_init, tq=512, kc=1024):
    _, batch, nh, seq, hd = qkv.shape
    kern = functools.partial(_diff_kernel, out_scale=1.0 - lambda_init, lambda_init=lambda_init,
                             tq=tq, kc=kc)
    spec = lambda which: pl.BlockSpec((None, None, None, seq, hd), lambda b, h: (which, b, h, 0, 0))
    return pl.pallas_call(
        kern,
        grid=(batch, nh),
        in_specs=[pl.BlockSpec((4, B_SUB_DIM), lambda b, h: (0, 0)),
                  spec(0), spec(1), spec(2),
                  pl.BlockSpec((1, hd), lambda b, h: (0, 0))],
        out_specs=pl.BlockSpec((None, seq, hd), lambda b, h: (b, 0, h)),
        out_shape=jax.ShapeDtypeStruct((batch, seq, nh * hd), BF16),
        scratch_shapes=[pltpu.VMEM((seq, 2 * hd), BF16), pltpu.VMEM((2 * tq, kc), F32)],
        compiler_params=_cparams(("parallel", "parallel")),
        name="diff",
    )(lam_params, qkv, qkv, qkv, out_gain.reshape(1, hd))


def _outproj_kernel(x_ref, a_ref, b_ref, wa_ref, wb_ref, o_ref):
    o_ref[...] = (x_ref[...]
                  + jnp.dot(a_ref[...], wa_ref[...], preferred_element_type=F32)
                  + jnp.dot(b_ref[...], wb_ref[...], preferred_element_type=F32))


def _outproj(x2d, a2d, b2d, w_out, *, tm=512):
    T, D = x2d.shape
    W = a2d.shape[1]
    return pl.pallas_call(
        _outproj_kernel,
        grid=(T // tm,),
        in_specs=[
            pl.BlockSpec((tm, D), lambda i: (i, 0)),
            pl.BlockSpec((tm, W), lambda i: (i, 0)),
            pl.BlockSpec((tm, W), lambda i: (i, 0)),
            pl.BlockSpec((W, D), lambda i: (0, 0)),
            pl.BlockSpec((W, D), lambda i: (1, 0)),
        ],
        out_specs=pl.BlockSpec((tm, D), lambda i: (i, 0)),
        out_shape=jax.ShapeDtypeStruct((T, D), F32),
        compiler_params=_cparams(("parallel",)),
        name="outproj",
    )(x2d, a2d, b2d, w_out, w_out)


def _layer(x, layer_idx, ffn1_norm, ffn1_w_in, ffn1_w_out, mix_norm, w_in,
           a_q_norm, a_k_norm, b_q_norm, b_k_norm,
           lambda_q1, lambda_k1, lambda_q2, lambda_k2,
           a_out_norm, b_out_norm, w_out, ffn2_norm, ffn2_w_in, ffn2_w_out):
    batch, seq, d_model = x.shape
    x2d = x.reshape(batch * seq, d_model)
    lambda_init = 0.8 - 0.6 * math.exp(-0.3 * layer_idx)

    x1 = _ffn(x2d, ffn1_norm, ffn1_w_in, ffn1_w_out)

    w_in16 = w_in.astype(BF16)
    qkv_a = _inproj(x1, mix_norm, w_in16, 0, a_q_norm, a_k_norm,
                    batch=batch, seq=seq, sub_dim=HEAD_DIM, out_dtype=F32)
    qkv_b = _inproj(x1, mix_norm, w_in16, 3 * GROUP_WIDTH, b_q_norm, b_k_norm,
                    batch=batch, seq=seq, sub_dim=B_SUB_DIM, out_dtype=BF16)

    a_o = _dilated(qkv_a, a_out_norm)
    lam_params = jnp.stack([lambda_q1, lambda_k1, lambda_q2, lambda_k2])
    b_o = _diff(qkv_b, lam_params, b_out_norm, lambda_init=lambda_init)

    x2 = _outproj(x1, a_o.reshape(batch * seq, GROUP_WIDTH), b_o.reshape(batch * seq, GROUP_WIDTH),
                  w_out.astype(BF16))
    out = _ffn(x2, ffn2_norm, ffn2_w_in, ffn2_w_out)
    return out.reshape(batch, seq, d_model)


def kernel(x, ffn1_norm, ffn1_w_in, ffn1_w_out, mix_norm, w_in, a_q_norm, a_k_norm, b_q_norm, b_k_norm,
           lambda_q1, lambda_k1, lambda_q2, lambda_k2, a_out_norm, b_out_norm, w_out,
           ffn2_norm, ffn2_w_in, ffn2_w_out):
    for l in range(ffn1_norm.shape[0]):
        x = _layer(x, l, ffn1_norm[l], ffn1_w_in[l], ffn1_w_out[l], mix_norm[l], w_in[l],
                   a_q_norm[l], a_k_norm[l], b_q_norm[l], b_k_norm[l],
                   lambda_q1[l], lambda_k1[l], lambda_q2[l], lambda_k2[l],
                   a_out_norm[l], b_out_norm[l], w_out[l],
                   ffn2_norm[l], ffn2_w_in[l], ffn2_w_out[l])
    return x
```

```python
import functools
import math

import numpy as np
import jax
import jax.numpy as jnp
from jax import lax
from jax.experimental import pallas as pl
from jax.experimental.pallas import tpu as pltpu

F32 = jnp.float32
BF16 = jnp.bfloat16

HEAD_DIM = 128
N_HEADS = 8
GROUP_WIDTH = N_HEADS * HEAD_DIM
B_SUB_DIM = 64
ROPE_THETA = 500000.0
ROPE_FRACTION = 4
PATTERNS = ((128, 1), (512, 4), (2048, 16))
DIL_STEP = 4
HALF_WIN = 64
EPS = 1e-6
NEG = -1e30
LOG2E = math.log2(math.e)

Q_BLK = 128
K_BLK = Q_BLK + 2 * HALF_WIN

VMEM_LIMIT = 60 * 1024 * 1024


def _cparams(sem):
    return pltpu.CompilerParams(dimension_semantics=sem, vmem_limit_bytes=VMEM_LIMIT)


def _ffn_kernel(x_ref, g_ref, wg_ref, wu_ref, wo_ref, o_ref, h_ref):
    j = pl.program_id(1)

    @pl.when(j == 0)
    def _():
        x = x_ref[...]
        ms = jnp.mean(x * x, axis=-1, keepdims=True)
        h_ref[...] = (x * lax.rsqrt(ms + EPS) * g_ref[...]).astype(BF16)
        o_ref[...] = x

    h = h_ref[...]
    gate = jnp.dot(h, wg_ref[...].astype(BF16), preferred_element_type=F32)
    up = jnp.dot(h, wu_ref[...].astype(BF16), preferred_element_type=F32)
    act = (gate * jax.nn.sigmoid(gate) * up * 0.5).astype(BF16)
    o_ref[...] += jnp.dot(act, wo_ref[...].astype(BF16), preferred_element_type=F32)


def _ffn(x2d, gain, w_in, w_out, *, tm=1024, tf=256):
    T, D = x2d.shape
    d_ff = w_out.shape[0]
    nj = d_ff // tf
    return pl.pallas_call(
        _ffn_kernel,
        grid=(T // tm, nj),
        in_specs=[
            pl.BlockSpec((tm, D), lambda i, j: (i, 0)),
            pl.BlockSpec((1, D), lambda i, j: (0, 0)),
            pl.BlockSpec((D, tf), lambda i, j: (0, j)),
            pl.BlockSpec((D, tf), lambda i, j: (0, j + nj)),
            pl.BlockSpec((tf, D), lambda i, j: (j, 0)),
        ],
        out_specs=pl.BlockSpec((tm, D), lambda i, j: (i, 0)),
        out_shape=jax.ShapeDtypeStruct((T, D), F32),
        scratch_shapes=[pltpu.VMEM((tm, D), BF16)],
        compiler_params=_cparams(("parallel", "arbitrary")),
        name="ffn",
    )(x2d, gain.reshape(1, D), w_in, w_in, w_out)


def _rope_tables(seq, sub_dim):
    rd = sub_dim // ROPE_FRACTION
    half = rd // 2
    inv = ROPE_THETA ** (-np.arange(0, rd, 2, dtype=np.float64) / rd)
    ang = np.arange(seq, dtype=np.float64)[:, None] * inv[None, :]
    cos, sin = np.cos(ang), np.sin(ang)
    c = np.ones((seq, sub_dim))
    s = np.zeros((seq, sub_dim))
    c[:, :half] = cos
    c[:, half:rd] = cos
    s[:, :half] = -sin
    s[:, half:rd] = sin
    reps = HEAD_DIM // sub_dim
    return (np.tile(c, (1, reps)).astype(np.float32), np.tile(s, (1, reps)).astype(np.float32))


def _inproj_kernel(x_ref, g_ref, w_ref, cos_a_ref, sin_a_ref, cos_b_ref, sin_b_ref, gain_ref,
                   oa_ref, ob_ref):
    x = x_ref[...]
    ms = jnp.mean(x * x, axis=-1, keepdims=True)
    h = (x * lax.rsqrt(ms + EPS) * g_ref[...]).astype(BF16)

    lane = lax.broadcasted_iota(jnp.int32, (1, HEAD_DIM), 1)
    lo = lane < B_SUB_DIM
    pair = 2 * HEAD_DIM
    pairs_per_group = GROUP_WIDTH // pair

    def project(col):
        return jnp.dot(h, w_ref[:, col:col + pair], preferred_element_type=F32)

    def qk_epilogue(ph, sub_dim, gain, cos, sin):
        half = sub_dim // ROPE_FRACTION // 2
        sq = ph * ph
        if sub_dim == HEAD_DIM:
            inv = lax.rsqrt(jnp.sum(sq, axis=-1, keepdims=True) * (1.0 / sub_dim) + EPS)
        else:
            ms_lo = jnp.sum(jnp.where(lo, sq, 0.0), axis=-1, keepdims=True) * (1.0 / sub_dim)
            ms_hi = jnp.sum(jnp.where(lo, 0.0, sq), axis=-1, keepdims=True) * (1.0 / sub_dim)
            inv = jnp.where(lo, lax.rsqrt(ms_lo + EPS), lax.rsqrt(ms_hi + EPS))
        y = ph * inv * gain
        rot = jnp.where(lane % sub_dim < half, pltpu.roll(y, HEAD_DIM - half, 1), pltpu.roll(y, half, 1))
        return y * cos + rot * sin

    groups = [(0, oa_ref, 0, HEAD_DIM, 0), (1, oa_ref, 1, HEAD_DIM, 1),
              (3, ob_ref, 0, B_SUB_DIM, 2), (4, ob_ref, 1, B_SUB_DIM, 3),
              (2, oa_ref, 2, None, None), (5, ob_ref, 2, None, None)]
    steps = [(grp, pr) for grp in groups for pr in range(pairs_per_group)]
    col_of = lambda step: step[0][0] * GROUP_WIDTH + step[1] * pair
    p_next = project(col_of(steps[0]))
    for n, ((_, out_ref, slot, sub_dim, gain_row), pr) in enumerate(steps):
        p, p_next = p_next, (project(col_of(steps[n + 1])) if n + 1 < len(steps) else None)
        for e in range(2):
            ph = p[:, e * HEAD_DIM:(e + 1) * HEAD_DIM]
            if sub_dim == HEAD_DIM:
                ph = qk_epilogue(ph, sub_dim, gain_ref[gain_row], cos_a_ref[...], sin_a_ref[...])
            elif sub_dim == B_SUB_DIM:
                ph = qk_epilogue(ph, sub_dim, gain_ref[gain_row], cos_b_ref[...], sin_b_ref[...])
            out_ref[slot, 2 * pr + e] = ph.astype(out_ref.dtype)


def _inproj(x2d, mix_gain, w_in, a_q_gain, a_k_gain, b_q_gain, b_k_gain, *, batch, seq, tm=512):
    T, D = x2d.shape
    spb = seq // tm
    cos_a, sin_a = _rope_tables(seq, HEAD_DIM)
    cos_b, sin_b = _rope_tables(seq, B_SUB_DIM)
    reps = HEAD_DIM // B_SUB_DIM
    gains = jnp.stack([a_q_gain * (LOG2E * HEAD_DIM ** -0.5), a_k_gain,
                       jnp.tile(b_q_gain, reps) * (LOG2E * B_SUB_DIM ** -0.5),
                       jnp.tile(b_k_gain, reps)]).reshape(4, 1, HEAD_DIM)
    table = lambda: pl.BlockSpec((tm, HEAD_DIM), lambda i: (i % spb, 0))
    out = lambda: pl.BlockSpec((3, None, N_HEADS, tm, HEAD_DIM), lambda i: (0, i // spb, 0, i % spb, 0))
    return pl.pallas_call(
        _inproj_kernel,
        grid=(T // tm,),
        in_specs=[
            pl.BlockSpec((tm, D), lambda i: (i, 0)),
            pl.BlockSpec((1, D), lambda i: (0, 0)),
            pl.BlockSpec(w_in.shape, lambda i: (0, 0), pipeline_mode=pl.Buffered(1)),
            table(), table(), table(), table(),
            pl.BlockSpec((4, 1, HEAD_DIM), lambda i: (0, 0, 0)),
        ],
        out_specs=[out(), out()],
        out_shape=[jax.ShapeDtypeStruct((3, batch, N_HEADS, seq, HEAD_DIM), F32),
                   jax.ShapeDtypeStruct((3, batch, N_HEADS, seq, HEAD_DIM), BF16)],
        compiler_params=_cparams(("parallel",)),
        name="inproj",
    )(x2d, mix_gain.reshape(1, D), w_in, jnp.asarray(cos_a), jnp.asarray(sin_a),
      jnp.asarray(cos_b), jnp.asarray(sin_b), gains)


def _band_bias():
    col_minus_row = np.arange(K_BLK)[None, :] - np.arange(Q_BLK)[:, None]
    return np.stack([np.where(np.abs(col_minus_row - lead) <= HALF_WIN, 0.0, NEG)
                     for lead in (0, HALF_WIN, 2 * HALF_WIN)]).astype(np.float32)


def _dilated_kernel(q_ref, k_ref, v_ref, bias_ref, g_ref, o_ref,
                    qs_ref, ks_ref, vs_ref, mid_ref, og_ref, lg_ref, *, seq):
    stage_rows = 256
    srcs, dsts = (q_ref, k_ref, v_ref), (qs_ref, ks_ref, vs_ref)
    (_, dil0), (_, dil1), (_, dil2) = PATTERNS
    assert dil0 == 1 and dil1 == DIL_STEP and dil2 == DIL_STEP * dil1 and seq // dil2 == stage_rows
    sub1 = seq // dil1

    def stage0(t, carry):
        rows = pl.ds(pl.multiple_of(t * stage_rows, stage_rows), stage_rows)
        for src, dst in zip(srcs, dsts):
            dst[0, rows, :] = src[rows, :].astype(BF16)
        return carry

    def stage1(t, carry):
        per_res = sub1 // stage_rows
        r = t // per_res
        c0 = (t % per_res) * stage_rows
        rows = pl.ds(pl.multiple_of(r * sub1 + c0, stage_rows), stage_rows)
        for a, (src, dst) in enumerate(zip(srcs, dsts)):
            x = src[pl.ds(r + dil1 * c0, stage_rows, stride=dil1), :]
            mid_ref[a, rows, :] = x
            dst[1, rows, :] = x.astype(BF16)
        return carry

    def stage2(t, carry):
        r1 = t // DIL_STEP
        rr = t % DIL_STEP
        rows = pl.ds(pl.multiple_of((dil1 * rr + r1) * stage_rows, stage_rows), stage_rows)
        for a, dst in enumerate(dsts):
            dst[2, rows, :] = mid_ref[a, pl.ds(r1 * sub1 + rr, stage_rows, stride=DIL_STEP), :].astype(BF16)
        return carry

    for stage in (stage0, stage1, stage2):
        lax.fori_loop(0, seq // stage_rows, stage, 0)

    unroll = 8
    for g, (_, dil) in enumerate(PATTERNS):
        sub_len = seq // dil
        nblk = sub_len // Q_BLK

        def body(it, carry, g=g, dil=dil, sub_len=sub_len, nblk=nblk):
            kvs, outs, scores = [], [], []
            for u in range(unroll):
                t = it * unroll + u
                r = t // nblk
                m0 = (t % nblk) * Q_BLK
                k0 = jnp.clip(m0 - HALF_WIN, 0, sub_len - K_BLK)
                base = r * sub_len
                q = qs_ref[g, pl.ds(pl.multiple_of(base + m0, Q_BLK), Q_BLK), :]
                kv = pl.ds(pl.multiple_of(base + k0, HALF_WIN), K_BLK)
                s = lax.dot_general(q, ks_ref[g, kv, :], (((1,), (1,)), ((), ())),
                                    preferred_element_type=F32)
                scores.append(s + bias_ref[(m0 - k0) // HALF_WIN])
                kvs.append(kv)
                if dil == 1:
                    outs.append(pl.ds(pl.multiple_of(m0, Q_BLK), Q_BLK))
                else:
                    outs.append(pl.ds(r + dil * m0, Q_BLK, stride=dil))
            s = jnp.concatenate(scores, axis=0)
            m = jnp.max(s, axis=-1, keepdims=True)
            p = jnp.exp2(s - m)
            l = jnp.sum(p, axis=-1, keepdims=True)
            p = p.astype(BF16)
            inv_l = 1.0 / l
            lse = jnp.broadcast_to(m + jnp.log2(l), (unroll * Q_BLK, HEAD_DIM))
            for u in range(unroll):
                blk = slice(u * Q_BLK, (u + 1) * Q_BLK)
                o = jnp.dot(p[blk], vs_ref[g, kvs[u], :], preferred_element_type=F32)
                og_ref[g, outs[u], :] = o * inv_l[blk]
                lg_ref[g, outs[u], :] = lse[blk]
            return carry

        lax.fori_loop(0, dil * nblk // unroll, body, 0)

    chunk = 256

    def comb(c, carry):
        rows = pl.ds(pl.multiple_of(c * chunk, chunk), chunk)
        l0, l1, l2 = lg_ref[0, rows, :], lg_ref[1, rows, :], lg_ref[2, rows, :]
        m = jnp.maximum(jnp.maximum(l0, l1), l2)
        w0, w1, w2 = jnp.exp2(l0 - m), jnp.exp2(l1 - m), jnp.exp2(l2 - m)
        o = (w0 * og_ref[0, rows, :] + w1 * og_ref[1, rows, :] + w2 * og_ref[2, rows, :]) / (w0 + w1 + w2)
        ms = jnp.mean(o * o, axis=-1, keepdims=True)
        o_ref[rows, :] = (o * lax.rsqrt(ms + EPS) * g_ref[...]).astype(o_ref.dtype)
        return carry

    lax.fori_loop(0, seq // chunk, comb, 0)


def _dilated(qkv, out_gain):
    _, batch, nh, seq, hd = qkv.shape
    npat = len(PATTERNS)
    spec = lambda which: pl.BlockSpec((None, None, None, seq, hd), lambda b, h: (which, b, h, 0, 0))
    return pl.pallas_call(
        functools.partial(_dilated_kernel, seq=seq),
        grid=(batch, nh),
        in_specs=[spec(0), spec(1), spec(2),
                  pl.BlockSpec((3, Q_BLK, K_BLK), lambda b, h: (0, 0, 0)),
                  pl.BlockSpec((1, hd), lambda b, h: (0, 0))],
        out_specs=pl.BlockSpec((None, seq, hd), lambda b, h: (b, 0, h)),
        out_shape=jax.ShapeDtypeStruct((batch, seq, nh * hd), BF16),
        scratch_shapes=[pltpu.VMEM((npat, seq, hd), BF16),
                        pltpu.VMEM((npat, seq, hd), BF16),
                        pltpu.VMEM((npat, seq, hd), BF16),
                        pltpu.VMEM((3, seq, hd), F32),
                        pltpu.VMEM((npat, seq, hd), F32),
                        pltpu.VMEM((npat, seq, hd), F32)],
        compiler_params=_cparams(("parallel", "parallel")),
        name="dilated",
    )(qkv, qkv, qkv, jnp.asarray(_band_bias()), out_gain.reshape(1, hd))


def _diff_kernel(lam_ref, q_ref, k_ref, v_ref, g_ref, o_ref, vaug_ref, s0_ref, *,
                 out_scale, lambda_init, tq, kc):
    seq = k_ref.shape[0]
    vaug_ref[:, :HEAD_DIM] = v_ref[...]
    vaug_ref[:, HEAD_DIM:] = jnp.ones((seq, HEAD_DIM), BF16)

    lp = lam_ref[...]
    lam = (jnp.exp(jnp.sum(lp[0:1] * lp[1:2], axis=-1, keepdims=True))
           - jnp.exp(jnp.sum(lp[2:3] * lp[3:4], axis=-1, keepdims=True)) + lambda_init)
    gain = g_ref[...] * out_scale
    lane = lax.broadcasted_iota(jnp.int32, (1, HEAD_DIM), 1)
    nc = seq // kc
    n_tiles = seq // tq

    def stacked_q(i):
        q = q_ref[pl.ds(pl.multiple_of(i * tq, tq), tq), :]
        zero = jnp.zeros_like(q)
        return jnp.concatenate([jnp.where(lane < B_SUB_DIM, q, zero),
                                jnp.where(lane < B_SUB_DIM, zero, q)], axis=0)

    def scores(q_st, c):
        return lax.dot_general(q_st, k_ref[c * kc:(c + 1) * kc, :], (((1,), (1,)), ((), ())),
                               preferred_element_type=F32)

    s0_ref[...] = scores(stacked_q(0), 0)

    def tile(i, carry):
        rows = pl.ds(pl.multiple_of(i * tq, tq), tq)
        q_st = stacked_q(i)
        m = acc = None
        s_next = s0_ref[...]
        for c in range(nc):
            s = s_next
            if c + 1 < nc:
                s_next = scores(q_st, c + 1)
            else:
                s0_ref[...] = scores(stacked_q(jnp.minimum(i + 1, n_tiles - 1)), 0)
            m_c = jnp.max(s, axis=-1, keepdims=True)
            m_new = m_c if m is None else jnp.maximum(m, m_c)
            pv = jnp.dot(jnp.exp2(s - m_new).astype(BF16), vaug_ref[c * kc:(c + 1) * kc, :],
                         preferred_element_type=F32)
            acc = pv if m is None else acc * jnp.exp2(m - m_new) + pv
            m = m_new
        o_st = acc[:, :HEAD_DIM] / acc[:, HEAD_DIM:]
        o = o_st[:tq] - lam * o_st[tq:]
        ms = jnp.mean(o * o, axis=-1, keepdims=True)
        o_ref[rows, :] = (o * lax.rsqrt(ms + EPS) * gain).astype(o_ref.dtype)
        return carry

    lax.fori_loop(0, n_tiles, tile, 0)


def _diff(qkv, lam_params, out_gain, *, lambda_init, tq=512, kc=1024):
    _, batch, nh, seq, hd = qkv.shape
    kern = functools.partial(_diff_kernel, out_scale=1.0 - lambda_init, lambda_init=lambda_init,
                             tq=tq, kc=kc)
    spec = lambda which: pl.BlockSpec((None, None, None, seq, hd), lambda b, h: (which, b, h, 0, 0))
    return pl.pallas_call(
        kern,
        grid=(batch, nh),
        in_specs=[pl.BlockSpec((4, B_SUB_DIM), lambda b, h: (0, 0)),
                  spec(0), spec(1), spec(2),
                  pl.BlockSpec((1, hd), lambda b, h: (0, 0))],
        out_specs=pl.BlockSpec((None, seq, hd), lambda b, h: (b, 0, h)),
        out_shape=jax.ShapeDtypeStruct((batch, seq, nh * hd), BF16),
        scratch_shapes=[pltpu.VMEM((seq, 2 * hd), BF16), pltpu.VMEM((2 * tq, kc), F32)],
        compiler_params=_cparams(("parallel", "parallel")),
        name="diff",
    )(lam_params, qkv, qkv, qkv, out_gain.reshape(1, hd))


def _outproj_kernel(x_ref, a_ref, b_ref, wa_ref, wb_ref, o_ref):
    o_ref[...] = (x_ref[...]
                  + jnp.dot(a_ref[...], wa_ref[...], preferred_element_type=F32)
                  + jnp.dot(b_ref[...], wb_ref[...], preferred_element_type=F32))


def _outproj(x2d, a2d, b2d, w_out, *, tm=512):
    T, D = x2d.shape
    W = a2d.shape[1]
    return pl.pallas_call(
        _outproj_kernel,
        grid=(T // tm,),
        in_specs=[
            pl.BlockSpec((tm, D), lambda i: (i, 0)),
            pl.BlockSpec((tm, W), lambda i: (i, 0)),
            pl.BlockSpec((tm, W), lambda i: (i, 0)),
            pl.BlockSpec((W, D), lambda i: (0, 0)),
            pl.BlockSpec((W, D), lambda i: (1, 0)),
        ],
        out_specs=pl.BlockSpec((tm, D), lambda i: (i, 0)),
        out_shape=jax.ShapeDtypeStruct((T, D), F32),
        compiler_params=_cparams(("parallel",)),
        name="outproj",
    )(x2d, a2d, b2d, w_out, w_out)


def _layer(x, layer_idx, ffn1_norm, ffn1_w_in, ffn1_w_out, mix_norm, w_in,
           a_q_norm, a_k_norm, b_q_norm, b_k_norm,
           lambda_q1, lambda_k1, lambda_q2, lambda_k2,
           a_out_norm, b_out_norm, w_out, ffn2_norm, ffn2_w_in, ffn2_w_out):
    batch, seq, d_model = x.shape
    x2d = x.reshape(batch * seq, d_model)
    lambda_init = 0.8 - 0.6 * math.exp(-0.3 * layer_idx)

    x1 = _ffn(x2d, ffn1_norm, ffn1_w_in, ffn1_w_out)

    qkv_a, qkv_b = _inproj(x1, mix_norm, w_in.astype(BF16), a_q_norm, a_k_norm, b_q_norm, b_k_norm,
                           batch=batch, seq=seq)

    a_o = _dilated(qkv_a, a_out_norm)
    lam_params = jnp.stack([lambda_q1, lambda_k1, lambda_q2, lambda_k2])
    b_o = _diff(qkv_b, lam_params, b_out_norm, lambda_init=lambda_init)

    x2 = _outproj(x1, a_o.reshape(batch * seq, GROUP_WIDTH), b_o.reshape(batch * seq, GROUP_WIDTH),
                  w_out.astype(BF16))
    out = _ffn(x2, ffn2_norm, ffn2_w_in, ffn2_w_out)
    return out.reshape(batch, seq, d_model)


def kernel(x, ffn1_norm, ffn1_w_in, ffn1_w_out, mix_norm, w_in, a_q_norm, a_k_norm, b_q_norm, b_k_norm,
           lambda_q1, lambda_k1, lambda_q2, lambda_k2, a_out_norm, b_out_norm, w_out,
           ffn2_norm, ffn2_w_in, ffn2_w_out):
    for l in range(ffn1_norm.shape[0]):
        x = _layer(x, l, ffn1_norm[l], ffn1_w_in[l], ffn1_w_out[l], mix_norm[l], w_in[l],
                   a_q_norm[l], a_k_norm[l], b_q_norm[l], b_k_norm[l],
                   lambda_q1[l], lambda_k1[l], lambda_q2[l], lambda_k2[l],
                   a_out_norm[l], b_out_norm[l], w_out[l],
                   ffn2_norm[l], ffn2_w_in[l], ffn2_w_out[l])
    return x
```

```python
import functools
import math

import numpy as np
import jax
import jax.numpy as jnp
from jax import lax
from jax.experimental import pallas as pl
from jax.experimental.pallas import tpu as pltpu

F32 = jnp.float32
BF16 = jnp.bfloat16

HEAD_DIM = 128
N_HEADS = 8
GROUP_WIDTH = N_HEADS * HEAD_DIM
B_SUB_DIM = 64
ROPE_THETA = 500000.0
ROPE_FRACTION = 4
PATTERNS = ((128, 1), (512, 4), (2048, 16))
DIL_STEP = 4
HALF_WIN = 64
EPS = 1e-6
NEG = -1e30
LOG2E = math.log2(math.e)

Q_BLK = 128
K_BLK = Q_BLK + 2 * HALF_WIN

BF16_SUBLANES = 16
VMEM_LIMIT = 60 * 1024 * 1024


def _cparams(sem):
    return pltpu.CompilerParams(dimension_semantics=sem, vmem_limit_bytes=VMEM_LIMIT)


def _ffn_kernel(*refs, n_side):
    x_ref, g_ref, wg_ref, wu_ref, wo_ref = refs[:5]
    side_in = refs[5:5 + n_side]
    o_ref = refs[5 + n_side]
    side_out = refs[6 + n_side:6 + 2 * n_side]
    h_ref = refs[6 + 2 * n_side]
    j = pl.program_id(1)

    @pl.when(j == 0)
    def _():
        x = x_ref[...]
        ms = jnp.mean(x * x, axis=-1, keepdims=True)
        h_ref[...] = (x * lax.rsqrt(ms + EPS) * g_ref[...]).astype(BF16)
        o_ref[...] = x

    for src, dst in zip(side_in, side_out):
        dst[...] = src[...].astype(BF16)

    h = h_ref[...]
    gate = jnp.dot(h, wg_ref[...].astype(BF16), preferred_element_type=F32)
    up = jnp.dot(h, wu_ref[...].astype(BF16), preferred_element_type=F32)
    act = (gate * jax.nn.sigmoid(gate) * up * 0.5).astype(BF16)
    o_ref[...] += jnp.dot(act, wo_ref[...].astype(BF16), preferred_element_type=F32)


def _ffn(x2d, gain, w_in, w_out, *, tm, tf, convert=()):
    T, D = x2d.shape
    d_ff = w_out.shape[0]
    nj = d_ff // tf
    n_steps = (T // tm) * nj
    side_specs, side_shapes = [], []
    for w in convert:
        rows = BF16_SUBLANES
        while w.shape[0] // rows > n_steps:
            rows += BF16_SUBLANES
        pieces = w.shape[0] // rows
        assert pieces * rows == w.shape[0], (w.shape, rows)
        side_specs.append(pl.BlockSpec(
            (rows, w.shape[1]), lambda i, j, pieces=pieces: (jnp.minimum(i * nj + j, pieces - 1), 0)))
        side_shapes.append(jax.ShapeDtypeStruct(w.shape, BF16))
    outs = pl.pallas_call(
        functools.partial(_ffn_kernel, n_side=len(convert)),
        grid=(T // tm, nj),
        in_specs=[
            pl.BlockSpec((tm, D), lambda i, j: (i, 0)),
            pl.BlockSpec((1, D), lambda i, j: (0, 0)),
            pl.BlockSpec((D, tf), lambda i, j: (0, j)),
            pl.BlockSpec((D, tf), lambda i, j: (0, j + nj)),
            pl.BlockSpec((tf, D), lambda i, j: (j, 0)),
        ] + side_specs,
        out_specs=[pl.BlockSpec((tm, D), lambda i, j: (i, 0))] + side_specs,
        out_shape=[jax.ShapeDtypeStruct((T, D), F32)] + side_shapes,
        scratch_shapes=[pltpu.VMEM((tm, D), BF16)],
        compiler_params=_cparams(("parallel", "arbitrary")),
        name="ffn",
    )(x2d, gain.reshape(1, D), w_in, w_in, w_out, *convert)
    return outs[0], tuple(outs[1:])


def _rope_tables(seq, sub_dim):
    rd = sub_dim // ROPE_FRACTION
    half = rd // 2
    inv = ROPE_THETA ** (-np.arange(0, rd, 2, dtype=np.float64) / rd)
    ang = np.arange(seq, dtype=np.float64)[:, None] * inv[None, :]
    cos, sin = np.cos(ang), np.sin(ang)
    c = np.ones((seq, sub_dim))
    s = np.zeros((seq, sub_dim))
    c[:, :half] = cos
    c[:, half:rd] = cos
    s[:, :half] = -sin
    s[:, half:rd] = sin
    reps = HEAD_DIM // sub_dim
    return (np.tile(c, (1, reps)).astype(np.float32), np.tile(s, (1, reps)).astype(np.float32))


def _inproj_kernel(x_ref, g_ref, w_ref, cos_a_ref, sin_a_ref, cos_b_ref, sin_b_ref, gain_ref,
                   oa_ref, ob_ref):
    x = x_ref[...]
    ms = jnp.mean(x * x, axis=-1, keepdims=True)
    h = (x * lax.rsqrt(ms + EPS) * g_ref[...]).astype(BF16)

    lane = lax.broadcasted_iota(jnp.int32, (1, HEAD_DIM), 1)
    lo = lane < B_SUB_DIM
    pair = 2 * HEAD_DIM
    pairs_per_group = GROUP_WIDTH // pair

    def project(col):
        return jnp.dot(h, w_ref[:, col:col + pair], preferred_element_type=F32)

    def qk_epilogue(ph, sub_dim, gain, cos, sin):
        half = sub_dim // ROPE_FRACTION // 2
        sq = ph * ph
        if sub_dim == HEAD_DIM:
            inv = lax.rsqrt(jnp.sum(sq, axis=-1, keepdims=True) * (1.0 / sub_dim) + EPS)
        else:
            ms_lo = jnp.sum(jnp.where(lo, sq, 0.0), axis=-1, keepdims=True) * (1.0 / sub_dim)
            ms_hi = jnp.sum(jnp.where(lo, 0.0, sq), axis=-1, keepdims=True) * (1.0 / sub_dim)
            inv = jnp.where(lo, lax.rsqrt(ms_lo + EPS), lax.rsqrt(ms_hi + EPS))
        y = ph * inv * gain
        rot = jnp.where(lane % sub_dim < half, pltpu.roll(y, HEAD_DIM - half, 1), pltpu.roll(y, half, 1))
        return y * cos + rot * sin

    groups = [(0, oa_ref, 0, HEAD_DIM, 0), (1, oa_ref, 1, HEAD_DIM, 1),
              (3, ob_ref, 0, B_SUB_DIM, 2), (4, ob_ref, 1, B_SUB_DIM, 3),
              (2, oa_ref, 2, None, None), (5, ob_ref, 2, None, None)]
    steps = [(grp, pr) for grp in groups for pr in range(pairs_per_group)]
    col_of = lambda step: step[0][0] * GROUP_WIDTH + step[1] * pair
    p_next = project(col_of(steps[0]))
    for n, ((_, out_ref, slot, sub_dim, gain_row), pr) in enumerate(steps):
        p, p_next = p_next, (project(col_of(steps[n + 1])) if n + 1 < len(steps) else None)
        for e in range(2):
            ph = p[:, e * HEAD_DIM:(e + 1) * HEAD_DIM]
            if sub_dim == HEAD_DIM:
                ph = qk_epilogue(ph, sub_dim, gain_ref[gain_row], cos_a_ref[...], sin_a_ref[...])
            elif sub_dim == B_SUB_DIM:
                ph = qk_epilogue(ph, sub_dim, gain_ref[gain_row], cos_b_ref[...], sin_b_ref[...])
            out_ref[slot, 2 * pr + e] = ph.astype(out_ref.dtype)


def _inproj(x2d, mix_gain, w_in, a_q_gain, a_k_gain, b_q_gain, b_k_gain, *, batch, seq, tm=512):
    T, D = x2d.shape
    spb = seq // tm
    cos_a, sin_a = _rope_tables(seq, HEAD_DIM)
    cos_b, sin_b = _rope_tables(seq, B_SUB_DIM)
    reps = HEAD_DIM // B_SUB_DIM
    gains = jnp.stack([a_q_gain * (LOG2E * HEAD_DIM ** -0.5), a_k_gain,
                       jnp.tile(b_q_gain, reps) * (LOG2E * B_SUB_DIM ** -0.5),
                       jnp.tile(b_k_gain, reps)]).reshape(4, 1, HEAD_DIM)
    table = lambda: pl.BlockSpec((tm, HEAD_DIM), lambda i: (i % spb, 0))
    out = lambda: pl.BlockSpec((3, None, N_HEADS, tm, HEAD_DIM), lambda i: (0, i // spb, 0, i % spb, 0))
    return pl.pallas_call(
        _inproj_kernel,
        grid=(T // tm,),
        in_specs=[
            pl.BlockSpec((tm, D), lambda i: (i, 0)),
            pl.BlockSpec((1, D), lambda i: (0, 0)),
            pl.BlockSpec(w_in.shape, lambda i: (0, 0), pipeline_mode=pl.Buffered(1)),
            table(), table(), table(), table(),
            pl.BlockSpec((4, 1, HEAD_DIM), lambda i: (0, 0, 0)),
        ],
        out_specs=[out(), out()],
        out_shape=[jax.ShapeDtypeStruct((3, batch, N_HEADS, seq, HEAD_DIM), F32),
                   jax.ShapeDtypeStruct((3, batch, N_HEADS, seq, HEAD_DIM), BF16)],
        compiler_params=_cparams(("parallel",)),
        name="inproj",
    )(x2d, mix_gain.reshape(1, D), w_in, jnp.asarray(cos_a), jnp.asarray(sin_a),
      jnp.asarray(cos_b), jnp.asarray(sin_b), gains)


def _band_bias():
    col_minus_row = np.arange(K_BLK)[None, :] - np.arange(Q_BLK)[:, None]
    return np.stack([np.where(np.abs(col_minus_row - lead) <= HALF_WIN, 0.0, NEG)
                     for lead in (0, HALF_WIN, 2 * HALF_WIN)]).astype(np.float32)


def _dilated_kernel(q_ref, k_ref, v_ref, bias_ref, g_ref, o_ref,
                    qs_ref, ks_ref, vs_ref, mid_ref, og_ref, lg_ref, *, seq):
    stage_rows = 256
    srcs, dsts = (q_ref, k_ref, v_ref), (qs_ref, ks_ref, vs_ref)
    (_, dil0), (_, dil1), (_, dil2) = PATTERNS
    assert dil0 == 1 and dil1 == DIL_STEP and dil2 == DIL_STEP * dil1 and seq // dil2 == stage_rows
    sub1 = seq // dil1

    def stage0(t, carry):
        rows = pl.ds(pl.multiple_of(t * stage_rows, stage_rows), stage_rows)
        for src, dst in zip(srcs, dsts):
            dst[0, rows, :] = src[rows, :].astype(BF16)
        return carry

    def stage1(t, carry):
        per_res = sub1 // stage_rows
        r = t // per_res
        c0 = (t % per_res) * stage_rows
        rows = pl.ds(pl.multiple_of(r * sub1 + c0, stage_rows), stage_rows)
        for a, (src, dst) in enumerate(zip(srcs, dsts)):
            x = src[pl.ds(r + dil1 * c0, stage_rows, stride=dil1), :]
            mid_ref[a, rows, :] = x
            dst[1, rows, :] = x.astype(BF16)
        return carry

    def stage2(t, carry):
        r1 = t // DIL_STEP
        rr = t % DIL_STEP
        rows = pl.ds(pl.multiple_of((dil1 * rr + r1) * stage_rows, stage_rows), stage_rows)
        for a, dst in enumerate(dsts):
            dst[2, rows, :] = mid_ref[a, pl.ds(r1 * sub1 + rr, stage_rows, stride=DIL_STEP), :].astype(BF16)
        return carry

    for stage in (stage0, stage1, stage2):
        lax.fori_loop(0, seq // stage_rows, stage, 0)

    unroll = 8
    for g, (_, dil) in enumerate(PATTERNS):
        sub_len = seq // dil
        nblk = sub_len // Q_BLK

        def body(it, carry, g=g, dil=dil, sub_len=sub_len, nblk=nblk):
            kvs, outs, scores = [], [], []
            for u in range(unroll):
                t = it * unroll + u
                r = t // nblk
                m0 = (t % nblk) * Q_BLK
                k0 = jnp.clip(m0 - HALF_WIN, 0, sub_len - K_BLK)
                base = r * sub_len
                q = qs_ref[g, pl.ds(pl.multiple_of(base + m0, Q_BLK), Q_BLK), :]
                kv = pl.ds(pl.multiple_of(base + k0, HALF_WIN), K_BLK)
                s = lax.dot_general(q, ks_ref[g, kv, :], (((1,), (1,)), ((), ())),
                                    preferred_element_type=F32)
                scores.append(s + bias_ref[(m0 - k0) // HALF_WIN])
                kvs.append(kv)
                if dil == 1:
                    outs.append(pl.ds(pl.multiple_of(m0, Q_BLK), Q_BLK))
                else:
                    outs.append(pl.ds(r + dil * m0, Q_BLK, stride=dil))
            s = jnp.concatenate(scores, axis=0)
            m = jnp.max(s, axis=-1, keepdims=True)
            p = jnp.exp2(s - m)
            l = jnp.sum(p, axis=-1, keepdims=True)
            p = p.astype(BF16)
            inv_l = 1.0 / l
            lse = jnp.broadcast_to(m + jnp.log2(l), (unroll * Q_BLK, HEAD_DIM))
            for u in range(unroll):
                blk = slice(u * Q_BLK, (u + 1) * Q_BLK)
                o = jnp.dot(p[blk], vs_ref[g, kvs[u], :], preferred_element_type=F32)
                og_ref[g, outs[u], :] = o * inv_l[blk]
                lg_ref[g, outs[u], :] = lse[blk]
            return carry

        lax.fori_loop(0, dil * nblk // unroll, body, 0)

    chunk = 256

    def comb(c, carry):
        rows = pl.ds(pl.multiple_of(c * chunk, chunk), chunk)
        l0, l1, l2 = lg_ref[0, rows, :], lg_ref[1, rows, :], lg_ref[2, rows, :]
        m = jnp.maximum(jnp.maximum(l0, l1), l2)
        w0, w1, w2 = jnp.exp2(l0 - m), jnp.exp2(l1 - m), jnp.exp2(l2 - m)
        o = (w0 * og_ref[0, rows, :] + w1 * og_ref[1, rows, :] + w2 * og_ref[2, rows, :]) / (w0 + w1 + w2)
        ms = jnp.mean(o * o, axis=-1, keepdims=True)
        o_ref[rows, :] = (o * lax.rsqrt(ms + EPS) * g_ref[...]).astype(o_ref.dtype)
        return carry

    lax.fori_loop(0, seq // chunk, comb, 0)


def _dilated(qkv, out_gain):
    _, batch, nh, seq, hd = qkv.shape
    npat = len(PATTERNS)
    spec = lambda which: pl.BlockSpec((None, None, None, seq, hd), lambda b, h: (which, b, h, 0, 0))
    return pl.pallas_call(
        functools.partial(_dilated_kernel, seq=seq),
        grid=(batch, nh),
        in_specs=[spec(0), spec(1), spec(2),
                  pl.BlockSpec((3, Q_BLK, K_BLK), lambda b, h: (0, 0, 0)),
                  pl.BlockSpec((1, hd), lambda b, h: (0, 0))],
        out_specs=pl.BlockSpec((None, seq, hd), lambda b, h: (b, 0, h)),
        out_shape=jax.ShapeDtypeStruct((batch, seq, nh * hd), BF16),
        scratch_shapes=[pltpu.VMEM((npat, seq, hd), BF16),
                        pltpu.VMEM((npat, seq, hd), BF16),
                        pltpu.VMEM((npat, seq, hd), BF16),
                        pltpu.VMEM((3, seq, hd), F32),
                        pltpu.VMEM((npat, seq, hd), F32),
                        pltpu.VMEM((npat, seq, hd), F32)],
        compiler_params=_cparams(("parallel", "parallel")),
        name="dilated",
    )(qkv, qkv, qkv, jnp.asarray(_band_bias()), out_gain.reshape(1, hd))


def _diff_kernel(lam_ref, q_ref, k_ref, v_ref, g_ref, o_ref, vaug_ref, s0_ref, *,
                 out_scale, lambda_init, tq, kc):
    seq = k_ref.shape[0]
    vaug_ref[:, :HEAD_DIM] = v_ref[...]
    vaug_ref[:, HEAD_DIM:] = jnp.ones((seq, HEAD_DIM), BF16)

    lp = lam_ref[...]
    lam = (jnp.exp(jnp.sum(lp[0:1] * lp[1:2], axis=-1, keepdims=True))
           - jnp.exp(jnp.sum(lp[2:3] * lp[3:4], axis=-1, keepdims=True)) + lambda_init)
    gain = g_ref[...] * out_scale
    lane = lax.broadcasted_iota(jnp.int32, (1, HEAD_DIM), 1)
    nc = seq // kc
    n_tiles = seq // tq

    def stacked_q(i):
        q = q_ref[pl.ds(pl.multiple_of(i * tq, tq), tq), :]
        zero = jnp.zeros_like(q)
        return jnp.concatenate([jnp.where(lane < B_SUB_DIM, q, zero),
                                jnp.where(lane < B_SUB_DIM, zero, q)], axis=0)

    def scores(q_st, c):
        return lax.dot_general(q_st, k_ref[c * kc:(c + 1) * kc, :], (((1,), (1,)), ((), ())),
                               preferred_element_type=F32)

    s0_ref[...] = scores(stacked_q(0), 0)

    def tile(i, carry):
        rows = pl.ds(pl.multiple_of(i * tq, tq), tq)
        q_st = stacked_q(i)
        m = acc = None
        s_next = s0_ref[...]
        for c in range(nc):
            s = s_next
            if c + 1 < nc:
                s_next = scores(q_st, c + 1)
            else:
                s0_ref[...] = scores(stacked_q(jnp.minimum(i + 1, n_tiles - 1)), 0)
            m_c = jnp.max(s, axis=-1, keepdims=True)
            m_new = m_c if m is None else jnp.maximum(m, m_c)
            pv = jnp.dot(jnp.exp2(s - m_new).astype(BF16), vaug_ref[c * kc:(c + 1) * kc, :],
                         preferred_element_type=F32)
            acc = pv if m is None else acc * jnp.exp2(m - m_new) + pv
            m = m_new
        o_st = acc[:, :HEAD_DIM] / acc[:, HEAD_DIM:]
        o = o_st[:tq] - lam * o_st[tq:]
        ms = jnp.mean(o * o, axis=-1, keepdims=True)
        o_ref[rows, :] = (o * lax.rsqrt(ms + EPS) * gain).astype(o_ref.dtype)
        return carry

    lax.fori_loop(0, n_tiles, tile, 0)


def _diff(qkv, lam_params, out_gain, *, lambda_init, tq=512, kc=1024):
    _, batch, nh, seq, hd = qkv.shape
    kern = functools.partial(_diff_kernel, out_scale=1.0 - lambda_init, lambda_init=lambda_init,
                             tq=tq, kc=kc)
    spec = lambda which: pl.BlockSpec((None, None, None, seq, hd), lambda b, h: (which, b, h, 0, 0))
    return pl.pallas_call(
        kern,
        grid=(batch, nh),
        in_specs=[pl.BlockSpec((4, B_SUB_DIM), lambda b, h: (0, 0)),
                  spec(0), spec(1), spec(2),
                  pl.BlockSpec((1, hd), lambda b, h: (0, 0))],
        out_specs=pl.BlockSpec((None, seq, hd), lambda b, h: (b, 0, h)),
        out_shape=jax.ShapeDtypeStruct((batch, seq, nh * hd), BF16),
        scratch_shapes=[pltpu.VMEM((seq, 2 * hd), BF16), pltpu.VMEM((2 * tq, kc), F32)],
        compiler_params=_cparams(("parallel", "parallel")),
        name="diff",
    )(lam_params, qkv, qkv, qkv, out_gain.reshape(1, hd))


def _outproj_kernel(x_ref, a_ref, b_ref, wa_ref, wb_ref, o_ref):
    o_ref[...] = (x_ref[...]
                  + jnp.dot(a_ref[...], wa_ref[...], preferred_element_type=F32)
                  + jnp.dot(b_ref[...], wb_ref[...], preferred_element_type=F32))


def _outproj(x2d, a2d, b2d, w_out, *, tm=512):
    T, D = x2d.shape
    W = a2d.shape[1]
    return pl.pallas_call(
        _outproj_kernel,
        grid=(T // tm,),
        in_specs=[
            pl.BlockSpec((tm, D), lambda i: (i, 0)),
            pl.BlockSpec((tm, W), lambda i: (i, 0)),
            pl.BlockSpec((tm, W), lambda i: (i, 0)),
            pl.BlockSpec((W, D), lambda i: (0, 0)),
            pl.BlockSpec((W, D), lambda i: (1, 0)),
        ],
        out_specs=pl.BlockSpec((tm, D), lambda i: (i, 0)),
        out_shape=jax.ShapeDtypeStruct((T, D), F32),
        compiler_params=_cparams(("parallel",)),
        name="outproj",
    )(x2d, a2d, b2d, w_out, w_out)


def _layer(x, layer_idx, ffn1_norm, ffn1_w_in, ffn1_w_out, mix_norm, w_in,
           a_q_norm, a_k_norm, b_q_norm, b_k_norm,
           lambda_q1, lambda_k1, lambda_q2, lambda_k2,
           a_out_norm, b_out_norm, w_out, ffn2_norm, ffn2_w_in, ffn2_w_out):
    batch, seq, d_model = x.shape
    x2d = x.reshape(batch * seq, d_model)
    lambda_init = 0.8 - 0.6 * math.exp(-0.3 * layer_idx)

    x1, (w_in16, w_out16, ffn2_w_in16, ffn2_w_out16) = _ffn(
        x2d, ffn1_norm, ffn1_w_in, ffn1_w_out, tm=1024, tf=256,
        convert=(w_in, w_out, ffn2_w_in, ffn2_w_out))

    qkv_a, qkv_b = _inproj(x1, mix_norm, w_in16, a_q_norm, a_k_norm, b_q_norm, b_k_norm,
                           batch=batch, seq=seq)

    a_o = _dilated(qkv_a, a_out_norm)
    lam_params = jnp.stack([lambda_q1, lambda_k1, lambda_q2, lambda_k2])
    b_o = _diff(qkv_b, lam_params, b_out_norm, lambda_init=lambda_init)

    x2 = _outproj(x1, a_o.reshape(batch * seq, GROUP_WIDTH), b_o.reshape(batch * seq, GROUP_WIDTH),
                  w_out16)
    out, _ = _ffn(x2, ffn2_norm, ffn2_w_in16, ffn2_w_out16, tm=1024, tf=512)
    return out.reshape(batch, seq, d_model)


def kernel(x, ffn1_norm, ffn1_w_in, ffn1_w_out, mix_norm, w_in, a_q_norm, a_k_norm, b_q_norm, b_k_norm,
           lambda_q1, lambda_k1, lambda_q2, lambda_k2, a_out_norm, b_out_norm, w_out,
           ffn2_norm, ffn2_w_in, ffn2_w_out):
    for l in range(ffn1_norm.shape[0]):
        x = _layer(x, l, ffn1_norm[l], ffn1_w_in[l], ffn1_w_out[l], mix_norm[l], w_in[l],
                   a_q_norm[l], a_k_norm[l], b_q_norm[l], b_k_norm[l],
                   lambda_q1[l], lambda_k1[l], lambda_q2[l], lambda_k2[l],
                   a_out_norm[l], b_out_norm[l], w_out[l],
                   ffn2_norm[l], ffn2_w_in[l], ffn2_w_out[l])
    return x
```

```python
import functools
import math

import numpy as np
import jax
import jax.numpy as jnp
from jax import lax
from jax.experimental import pallas as pl
from jax.experimental.pallas import tpu as pltpu

F32 = jnp.float32
BF16 = jnp.bfloat16

HEAD_DIM = 128
N_HEADS = 8
GROUP_WIDTH = N_HEADS * HEAD_DIM
B_SUB_DIM = 64
ROPE_THETA = 500000.0
ROPE_FRACTION = 4
PATTERNS = ((128, 1), (512, 4), (2048, 16))
DIL_STEP = 4
HALF_WIN = 64
EPS = 1e-6
NEG = -1e30
LOG2E = math.log2(math.e)

Q_BLK = 128
K_BLK = Q_BLK + 2 * HALF_WIN

BF16_SUBLANES = 16
VMEM_LIMIT = 60 * 1024 * 1024


def _cparams(sem):
    return pltpu.CompilerParams(dimension_semantics=sem, vmem_limit_bytes=VMEM_LIMIT)


def _ffn_kernel(*refs, n_side, has_prev, emit_weights):
    x_ref, g_ref, wg_ref, wu_ref, wo_ref = refs[:5]
    pos = 5
    side_in = refs[pos:pos + n_side]
    pos += n_side + (1 if has_prev else 0)
    o_ref = refs[pos]
    side_out = refs[pos + 1:pos + 1 + n_side]
    pos += 1 + n_side
    own16 = refs[pos:pos + 3] if emit_weights else ()
    h_ref = refs[-1]
    j = pl.program_id(1)

    @pl.when(j == 0)
    def _():
        x = x_ref[...]
        ms = jnp.mean(x * x, axis=-1, keepdims=True)
        h_ref[...] = (x * lax.rsqrt(ms + EPS) * g_ref[...]).astype(BF16)
        o_ref[...] = x

    for src, dst in zip(side_in, side_out):
        dst[...] = src[...].astype(BF16)

    weights = [w_ref[...].astype(BF16) for w_ref in (wg_ref, wu_ref, wo_ref)]
    for w16, dst in zip(weights, own16):
        dst[...] = w16
    wg, wu, wo = weights
    h = h_ref[...]
    gate = jnp.dot(h, wg, preferred_element_type=F32)
    up = jnp.dot(h, wu, preferred_element_type=F32)
    act = (gate * jax.nn.sigmoid(gate) * up * 0.5).astype(BF16)
    o_ref[...] += jnp.dot(act, wo, preferred_element_type=F32)


def _convert_rows(n_rows, n_steps):
    rows = BF16_SUBLANES
    while n_rows % rows or n_rows // rows > n_steps:
        rows += BF16_SUBLANES
    return rows


def _ffn(x2d, gain, wg, wu, wo, *, tm, tf, up_block_offset, first_tile, n_tiles,
         prev_out=None, convert=(), emit_weights=False):
    T, D = x2d.shape
    d_ff = wo.shape[0]
    nj = d_ff // tf
    n_steps = n_tiles * nj
    side_specs, side_shapes = [], []
    for w in convert:
        rows = _convert_rows(w.shape[0], n_steps)
        pieces = w.shape[0] // rows
        side_specs.append(pl.BlockSpec(
            (rows, w.shape[1]), lambda i, j, pieces=pieces: (jnp.minimum(i * nj + j, pieces - 1), 0)))
        side_shapes.append(jax.ShapeDtypeStruct(w.shape, BF16))
    own_specs, own_shapes = [], []
    if emit_weights:
        assert n_tiles == 1
        own_specs = [pl.BlockSpec((D, tf), lambda i, j: (0, j)), pl.BlockSpec((D, tf), lambda i, j: (0, j)),
                     pl.BlockSpec((tf, D), lambda i, j: (j, 0))]
        own_shapes = [jax.ShapeDtypeStruct((D, d_ff), BF16), jax.ShapeDtypeStruct((D, d_ff), BF16),
                      jax.ShapeDtypeStruct((d_ff, D), BF16)]
    prev_specs = [] if prev_out is None else [pl.BlockSpec(memory_space=pl.ANY)]
    prev_args = [] if prev_out is None else [prev_out]
    n_in = 5 + len(convert)
    outs = pl.pallas_call(
        functools.partial(_ffn_kernel, n_side=len(convert), has_prev=prev_out is not None,
                          emit_weights=emit_weights),
        grid=(n_tiles, nj),
        in_specs=[
            pl.BlockSpec((tm, D), lambda i, j: (i + first_tile, 0)),
            pl.BlockSpec((1, D), lambda i, j: (0, 0)),
            pl.BlockSpec((D, tf), lambda i, j: (0, j)),
            pl.BlockSpec((D, tf), lambda i, j: (0, j + up_block_offset)),
            pl.BlockSpec((tf, D), lambda i, j: (j, 0)),
        ] + side_specs + prev_specs,
        out_specs=[pl.BlockSpec((tm, D), lambda i, j: (i + first_tile, 0))] + side_specs + own_specs,
        out_shape=[jax.ShapeDtypeStruct((T, D), F32)] + side_shapes + own_shapes,
        input_output_aliases={} if prev_out is None else {n_in: 0},
        scratch_shapes=[pltpu.VMEM((tm, D), BF16)],
        compiler_params=_cparams(("parallel", "arbitrary")),
        name="ffn",
    )(x2d, gain.reshape(1, D), wg, wu, wo, *convert, *prev_args)
    n_side = len(convert)
    return outs[0], tuple(outs[1:1 + n_side]), tuple(outs[1 + n_side:])


def _ffn_split(x2d, gain, w_in, w_out, *, tm, convert=()):
    T = x2d.shape[0]
    d_ff = w_out.shape[0]
    tf_f32, tf_bf16 = 256, 512
    head, _, (wg16, wu16, wo16) = _ffn(
        x2d, gain, w_in, w_in, w_out, tm=tm, tf=tf_f32, up_block_offset=d_ff // tf_f32,
        first_tile=0, n_tiles=1, emit_weights=True)
    out, side, _ = _ffn(
        x2d, gain, wg16, wu16, wo16, tm=tm, tf=tf_bf16, up_block_offset=0,
        first_tile=1, n_tiles=T // tm - 1, prev_out=head, convert=convert)
    return out, side


def _rope_tables(seq, sub_dim):
    rd = sub_dim // ROPE_FRACTION
    half = rd // 2
    inv = ROPE_THETA ** (-np.arange(0, rd, 2, dtype=np.float64) / rd)
    ang = np.arange(seq, dtype=np.float64)[:, None] * inv[None, :]
    cos, sin = np.cos(ang), np.sin(ang)
    c = np.ones((seq, sub_dim))
    s = np.zeros((seq, sub_dim))
    c[:, :half] = cos
    c[:, half:rd] = cos
    s[:, :half] = -sin
    s[:, half:rd] = sin
    reps = HEAD_DIM // sub_dim
    return (np.tile(c, (1, reps)).astype(np.float32), np.tile(s, (1, reps)).astype(np.float32))


def _inproj_kernel(x_ref, g_ref, w_ref, cos_a_ref, sin_a_ref, cos_b_ref, sin_b_ref, gain_ref,
                   oa_ref, ob_ref):
    x = x_ref[...]
    ms = jnp.mean(x * x, axis=-1, keepdims=True)
    h = (x * lax.rsqrt(ms + EPS) * g_ref[...]).astype(BF16)

    lane = lax.broadcasted_iota(jnp.int32, (1, HEAD_DIM), 1)
    lo = lane < B_SUB_DIM
    pair = 2 * HEAD_DIM
    pairs_per_group = GROUP_WIDTH // pair

    def project(col):
        return jnp.dot(h, w_ref[:, col:col + pair], preferred_element_type=F32)

    def qk_epilogue(ph, sub_dim, gain, cos, sin):
        half = sub_dim // ROPE_FRACTION // 2
        sq = ph * ph
        if sub_dim == HEAD_DIM:
            inv = lax.rsqrt(jnp.sum(sq, axis=-1, keepdims=True) * (1.0 / sub_dim) + EPS)
        else:
            ms_lo = jnp.sum(jnp.where(lo, sq, 0.0), axis=-1, keepdims=True) * (1.0 / sub_dim)
            ms_hi = jnp.sum(jnp.where(lo, 0.0, sq), axis=-1, keepdims=True) * (1.0 / sub_dim)
            inv = jnp.where(lo, lax.rsqrt(ms_lo + EPS), lax.rsqrt(ms_hi + EPS))
        y = ph * inv * gain
        rot = jnp.where(lane % sub_dim < half, pltpu.roll(y, HEAD_DIM - half, 1), pltpu.roll(y, half, 1))
        return y * cos + rot * sin

    groups = [(0, oa_ref, 0, HEAD_DIM, 0), (1, oa_ref, 1, HEAD_DIM, 1),
              (3, ob_ref, 0, B_SUB_DIM, 2), (4, ob_ref, 1, B_SUB_DIM, 3),
              (2, oa_ref, 2, None, None), (5, ob_ref, 2, None, None)]
    steps = [(grp, pr) for grp in groups for pr in range(pairs_per_group)]
    col_of = lambda step: step[0][0] * GROUP_WIDTH + step[1] * pair
    p_next = project(col_of(steps[0]))
    for n, ((_, out_ref, slot, sub_dim, gain_row), pr) in enumerate(steps):
        p, p_next = p_next, (project(col_of(steps[n + 1])) if n + 1 < len(steps) else None)
        for e in range(2):
            ph = p[:, e * HEAD_DIM:(e + 1) * HEAD_DIM]
            if sub_dim == HEAD_DIM:
                ph = qk_epilogue(ph, sub_dim, gain_ref[gain_row], cos_a_ref[...], sin_a_ref[...])
            elif sub_dim == B_SUB_DIM:
                ph = qk_epilogue(ph, sub_dim, gain_ref[gain_row], cos_b_ref[...], sin_b_ref[...])
            out_ref[slot, 2 * pr + e] = ph.astype(out_ref.dtype)


def _inproj(x2d, mix_gain, w_in, a_q_gain, a_k_gain, b_q_gain, b_k_gain, *, batch, seq, tm=512):
    T, D = x2d.shape
    spb = seq // tm
    cos_a, sin_a = _rope_tables(seq, HEAD_DIM)
    cos_b, sin_b = _rope_tables(seq, B_SUB_DIM)
    reps = HEAD_DIM // B_SUB_DIM
    gains = jnp.stack([a_q_gain * (LOG2E * HEAD_DIM ** -0.5), a_k_gain,
                       jnp.tile(b_q_gain, reps) * (LOG2E * B_SUB_DIM ** -0.5),
                       jnp.tile(b_k_gain, reps)]).reshape(4, 1, HEAD_DIM)
    table = lambda: pl.BlockSpec((tm, HEAD_DIM), lambda i: (i % spb, 0))
    out = lambda: pl.BlockSpec((3, None, N_HEADS, tm, HEAD_DIM), lambda i: (0, i // spb, 0, i % spb, 0))
    return pl.pallas_call(
        _inproj_kernel,
        grid=(T // tm,),
        in_specs=[
            pl.BlockSpec((tm, D), lambda i: (i, 0)),
            pl.BlockSpec((1, D), lambda i: (0, 0)),
            pl.BlockSpec(w_in.shape, lambda i: (0, 0), pipeline_mode=pl.Buffered(1)),
            table(), table(), table(), table(),
            pl.BlockSpec((4, 1, HEAD_DIM), lambda i: (0, 0, 0)),
        ],
        out_specs=[out(), out()],
        out_shape=[jax.ShapeDtypeStruct((3, batch, N_HEADS, seq, HEAD_DIM), F32),
                   jax.ShapeDtypeStruct((3, batch, N_HEADS, seq, HEAD_DIM), BF16)],
        compiler_params=_cparams(("parallel",)),
        name="inproj",
    )(x2d, mix_gain.reshape(1, D), w_in, jnp.asarray(cos_a), jnp.asarray(sin_a),
      jnp.asarray(cos_b), jnp.asarray(sin_b), gains)


def _band_bias():
    col_minus_row = np.arange(K_BLK)[None, :] - np.arange(Q_BLK)[:, None]
    return np.stack([np.where(np.abs(col_minus_row - lead) <= HALF_WIN, 0.0, NEG)
                     for lead in (0, HALF_WIN, 2 * HALF_WIN)]).astype(np.float32)


def _dilated_kernel(q_ref, k_ref, v_ref, bias_ref, g_ref, o_ref,
                    qs_ref, ks_ref, vs_ref, mid_ref, og_ref, lg_ref, *, seq):
    stage_rows = 256
    srcs, dsts = (q_ref, k_ref, v_ref), (qs_ref, ks_ref, vs_ref)
    (_, dil0), (_, dil1), (_, dil2) = PATTERNS
    assert dil0 == 1 and dil1 == DIL_STEP and dil2 == DIL_STEP * dil1 and seq // dil2 == stage_rows
    sub1 = seq // dil1

    def stage0(t, carry):
        rows = pl.ds(pl.multiple_of(t * stage_rows, stage_rows), stage_rows)
        for src, dst in zip(srcs, dsts):
            dst[0, rows, :] = src[rows, :].astype(BF16)
        return carry

    def stage1(t, carry):
        per_res = sub1 // stage_rows
        r = t // per_res
        c0 = (t % per_res) * stage_rows
        rows = pl.ds(pl.multiple_of(r * sub1 + c0, stage_rows), stage_rows)
        for a, (src, dst) in enumerate(zip(srcs, dsts)):
            x = src[pl.ds(r + dil1 * c0, stage_rows, stride=dil1), :]
            mid_ref[a, rows, :] = x
            dst[1, rows, :] = x.astype(BF16)
        return carry

    def stage2(t, carry):
        r1 = t // DIL_STEP
        rr = t % DIL_STEP
        rows = pl.ds(pl.multiple_of((dil1 * rr + r1) * stage_rows, stage_rows), stage_rows)
        for a, dst in enumerate(dsts):
            dst[2, rows, :] = mid_ref[a, pl.ds(r1 * sub1 + rr, stage_rows, stride=DIL_STEP), :].astype(BF16)
        return carry

    for stage in (stage0, stage1, stage2):
        lax.fori_loop(0, seq // stage_rows, stage, 0)

    unroll = 8
    for g, (_, dil) in enumerate(PATTERNS):
        sub_len = seq // dil
        nblk = sub_len // Q_BLK

        def body(it, carry, g=g, dil=dil, sub_len=sub_len, nblk=nblk):
            kvs, outs, scores = [], [], []
            for u in range(unroll):
                t = it * unroll + u
                r = t // nblk
                m0 = (t % nblk) * Q_BLK
                k0 = jnp.clip(m0 - HALF_WIN, 0, sub_len - K_BLK)
                base = r * sub_len
                q = qs_ref[g, pl.ds(pl.multiple_of(base + m0, Q_BLK), Q_BLK), :]
                kv = pl.ds(pl.multiple_of(base + k0, HALF_WIN), K_BLK)
                s = lax.dot_general(q, ks_ref[g, kv, :], (((1,), (1,)), ((), ())),
                                    preferred_element_type=F32)
                scores.append(s + bias_ref[(m0 - k0) // HALF_WIN])
                kvs.append(kv)
                if dil == 1:
                    outs.append(pl.ds(pl.multiple_of(m0, Q_BLK), Q_BLK))
                else:
                    outs.append(pl.ds(r + dil * m0, Q_BLK, stride=dil))
            s = jnp.concatenate(scores, axis=0)
            m = jnp.max(s, axis=-1, keepdims=True)
            p = jnp.exp2(s - m)
            l = jnp.sum(p, axis=-1, keepdims=True)
            p = p.astype(BF16)
            inv_l = 1.0 / l
            lse = jnp.broadcast_to(m + jnp.log2(l), (unroll * Q_BLK, HEAD_DIM))
            for u in range(unroll):
                blk = slice(u * Q_BLK, (u + 1) * Q_BLK)
                o = jnp.dot(p[blk], vs_ref[g, kvs[u], :], preferred_element_type=F32)
                og_ref[g, outs[u], :] = o * inv_l[blk]
                lg_ref[g, outs[u], :] = lse[blk]
            return carry

        lax.fori_loop(0, dil * nblk // unroll, body, 0)

    chunk = 256

    def comb(c, carry):
        rows = pl.ds(pl.multiple_of(c * chunk, chunk), chunk)
        l0, l1, l2 = lg_ref[0, rows, :], lg_ref[1, rows, :], lg_ref[2, rows, :]
        m = jnp.maximum(jnp.maximum(l0, l1), l2)
        w0, w1, w2 = jnp.exp2(l0 - m), jnp.exp2(l1 - m), jnp.exp2(l2 - m)
        o = (w0 * og_ref[0, rows, :] + w1 * og_ref[1, rows, :] + w2 * og_ref[2, rows, :]) / (w0 + w1 + w2)
        ms = jnp.mean(o * o, axis=-1, keepdims=True)
        o_ref[rows, :] = (o * lax.rsqrt(ms + EPS) * g_ref[...]).astype(o_ref.dtype)
        return carry

    lax.fori_loop(0, seq // chunk, comb, 0)


def _dilated(qkv, out_gain):
    _, batch, nh, seq, hd = qkv.shape
    npat = len(PATTERNS)
    spec = lambda which: pl.BlockSpec((None, None, None, seq, hd), lambda b, h: (which, b, h, 0, 0))
    return pl.pallas_call(
        functools.partial(_dilated_kernel, seq=seq),
        grid=(batch, nh),
        in_specs=[spec(0), spec(1), spec(2),
                  pl.BlockSpec((3, Q_BLK, K_BLK), lambda b, h: (0, 0, 0)),
                  pl.BlockSpec((1, hd), lambda b, h: (0, 0))],
        out_specs=pl.BlockSpec((None, seq, hd), lambda b, h: (b, 0, h)),
        out_shape=jax.ShapeDtypeStruct((batch, seq, nh * hd), BF16),
        scratch_shapes=[pltpu.VMEM((npat, seq, hd), BF16),
                        pltpu.VMEM((npat, seq, hd), BF16),
                        pltpu.VMEM((npat, seq, hd), BF16),
                        pltpu.VMEM((3, seq, hd), F32),
                        pltpu.VMEM((npat, seq, hd), F32),
                        pltpu.VMEM((npat, seq, hd), F32)],
        compiler_params=_cparams(("parallel", "parallel")),
        name="dilated",
    )(qkv, qkv, qkv, jnp.asarray(_band_bias()), out_gain.reshape(1, hd))


def _diff_kernel(lam_ref, q_ref, k_ref, v_ref, g_ref, o_ref, vaug_ref, s0_ref, *,
                 out_scale, lambda_init, tq, kc):
    seq = k_ref.shape[0]
    vaug_ref[:, :HEAD_DIM] = v_ref[...]
    vaug_ref[:, HEAD_DIM:] = jnp.ones((seq, HEAD_DIM), BF16)

    lp = lam_ref[...]
    lam = (jnp.exp(jnp.sum(lp[0:1] * lp[1:2], axis=-1, keepdims=True))
           - jnp.exp(jnp.sum(lp[2:3] * lp[3:4], axis=-1, keepdims=True)) + lambda_init)
    gain = g_ref[...] * out_scale
    lane = lax.broadcasted_iota(jnp.int32, (1, HEAD_DIM), 1)
    nc = seq // kc
    n_tiles = seq // tq

    def stacked_q(i):
        q = q_ref[pl.ds(pl.multiple_of(i * tq, tq), tq), :]
        zero = jnp.zeros_like(q)
        return jnp.concatenate([jnp.where(lane < B_SUB_DIM, q, zero),
                                jnp.where(lane < B_SUB_DIM, zero, q)], axis=0)

    def scores(q_st, c):
        return lax.dot_general(q_st, k_ref[c * kc:(c + 1) * kc, :], (((1,), (1,)), ((), ())),
                               preferred_element_type=F32)

    s0_ref[...] = scores(stacked_q(0), 0)

    def tile(i, carry):
        rows = pl.ds(pl.multiple_of(i * tq, tq), tq)
        q_st = stacked_q(i)
        m = acc = None
        s_next = s0_ref[...]
        for c in range(nc):
            s = s_next
            if c + 1 < nc:
                s_next = scores(q_st, c + 1)
            else:
                s0_ref[...] = scores(stacked_q(jnp.minimum(i + 1, n_tiles - 1)), 0)
            m_c = jnp.max(s, axis=-1, keepdims=True)
            m_new = m_c if m is None else jnp.maximum(m, m_c)
            pv = jnp.dot(jnp.exp2(s - m_new).astype(BF16), vaug_ref[c * kc:(c + 1) * kc, :],
                         preferred_element_type=F32)
            acc = pv if m is None else acc * jnp.exp2(m - m_new) + pv
            m = m_new
        o_st = acc[:, :HEAD_DIM] / acc[:, HEAD_DIM:]
        o = o_st[:tq] - lam * o_st[tq:]
        ms = jnp.mean(o * o, axis=-1, keepdims=True)
        o_ref[rows, :] = (o * lax.rsqrt(ms + EPS) * gain).astype(o_ref.dtype)
        return carry

    lax.fori_loop(0, n_tiles, tile, 0)


def _diff(qkv, lam_params, out_gain, *, lambda_init, tq=512, kc=1024):
    _, batch, nh, seq, hd = qkv.shape
    kern = functools.partial(_diff_kernel, out_scale=1.0 - lambda_init, lambda_init=lambda_init,
                             tq=tq, kc=kc)
    spec = lambda which: pl.BlockSpec((None, None, None, seq, hd), lambda b, h: (which, b, h, 0, 0))
    return pl.pallas_call(
        kern,
        grid=(batch, nh),
        in_specs=[pl.BlockSpec((4, B_SUB_DIM), lambda b, h: (0, 0)),
                  spec(0), spec(1), spec(2),
                  pl.BlockSpec((1, hd), lambda b, h: (0, 0))],
        out_specs=pl.BlockSpec((None, seq, hd), lambda b, h: (b, 0, h)),
        out_shape=jax.ShapeDtypeStruct((batch, seq, nh * hd), BF16),
        scratch_shapes=[pltpu.VMEM((seq, 2 * hd), BF16), pltpu.VMEM((2 * tq, kc), F32)],
        compiler_params=_cparams(("parallel", "parallel")),
        name="diff",
    )(lam_params, qkv, qkv, qkv, out_gain.reshape(1, hd))


def _outproj_kernel(x_ref, a_ref, b_ref, wa_ref, wb_ref, o_ref):
    o_ref[...] = (x_ref[...]
                  + jnp.dot(a_ref[...], wa_ref[...], preferred_element_type=F32)
                  + jnp.dot(b_ref[...], wb_ref[...], preferred_element_type=F32))


def _outproj(x2d, a2d, b2d, w_out, *, tm=512):
    T, D = x2d.shape
    W = a2d.shape[1]
    return pl.pallas_call(
        _outproj_kernel,
        grid=(T // tm,),
        in_specs=[
            pl.BlockSpec((tm, D), lambda i: (i, 0)),
            pl.BlockSpec((tm, W), lambda i: (i, 0)),
            pl.BlockSpec((tm, W), lambda i: (i, 0)),
            pl.BlockSpec((W, D), lambda i: (0, 0)),
            pl.BlockSpec((W, D), lambda i: (1, 0)),
        ],
        out_specs=pl.BlockSpec((tm, D), lambda i: (i, 0)),
        out_shape=jax.ShapeDtypeStruct((T, D), F32),
        compiler_params=_cparams(("parallel",)),
        name="outproj",
    )(x2d, a2d, b2d, w_out, w_out)


def _layer(x, layer_idx, ffn1_norm, ffn1_w_in, ffn1_w_out, mix_norm, w_in,
           a_q_norm, a_k_norm, b_q_norm, b_k_norm,
           lambda_q1, lambda_k1, lambda_q2, lambda_k2,
           a_out_norm, b_out_norm, w_out, ffn2_norm, ffn2_w_in, ffn2_w_out):
    batch, seq, d_model = x.shape
    x2d = x.reshape(batch * seq, d_model)
    lambda_init = 0.8 - 0.6 * math.exp(-0.3 * layer_idx)

    x1, (w_in16, w_out16) = _ffn_split(x2d, ffn1_norm, ffn1_w_in, ffn1_w_out, tm=1024,
                                       convert=(w_in, w_out))

    qkv_a, qkv_b = _inproj(x1, mix_norm, w_in16, a_q_norm, a_k_norm, b_q_norm, b_k_norm,
                           batch=batch, seq=seq)

    a_o = _dilated(qkv_a, a_out_norm)
    lam_params = jnp.stack([lambda_q1, lambda_k1, lambda_q2, lambda_k2])
    b_o = _diff(qkv_b, lam_params, b_out_norm, lambda_init=lambda_init)

    x2 = _outproj(x1, a_o.reshape(batch * seq, GROUP_WIDTH), b_o.reshape(batch * seq, GROUP_WIDTH),
                  w_out16)
    out, _ = _ffn_split(x2, ffn2_norm, ffn2_w_in, ffn2_w_out, tm=1024)
    return out.reshape(batch, seq, d_model)


def kernel(x, ffn1_norm, ffn1_w_in, ffn1_w_out, mix_norm, w_in, a_q_norm, a_k_norm, b_q_norm, b_k_norm,
           lambda_q1, lambda_k1, lambda_q2, lambda_k2, a_out_norm, b_out_norm, w_out,
           ffn2_norm, ffn2_w_in, ffn2_w_out):
    for l in range(ffn1_norm.shape[0]):
        x = _layer(x, l, ffn1_norm[l], ffn1_w_in[l], ffn1_w_out[l], mix_norm[l], w_in[l],
                   a_q_norm[l], a_k_norm[l], b_q_norm[l], b_k_norm[l],
                   lambda_q1[l], lambda_k1[l], lambda_q2[l], lambda_k2[l],
                   a_out_norm[l], b_out_norm[l], w_out[l],
                   ffn2_norm[l], ffn2_w_in[l], ffn2_w_out[l])
    return x
```

```python
import functools
import math

import numpy as np
import jax
import jax.numpy as jnp
from jax import lax
from jax.experimental import pallas as pl
from jax.experimental.pallas import tpu as pltpu

F32 = jnp.float32
BF16 = jnp.bfloat16

HEAD_DIM = 128
N_HEADS = 8
GROUP_WIDTH = N_HEADS * HEAD_DIM
B_SUB_DIM = 64
ROPE_THETA = 500000.0
ROPE_FRACTION = 4
PATTERNS = ((128, 1), (512, 4), (2048, 16))
DIL_STEP = 4
HALF_WIN = 64
EPS = 1e-6
NEG = -1e30
LOG2E = math.log2(math.e)

Q_BLK = 128
K_BLK = Q_BLK + 2 * HALF_WIN

BF16_SUBLANES = 16
VMEM_LIMIT = 60 * 1024 * 1024


def _cparams(sem):
    return pltpu.CompilerParams(dimension_semantics=sem, vmem_limit_bytes=VMEM_LIMIT)


def _ffn_kernel(*refs, n_side, has_prev, emit_weights):
    x_ref, g_ref, wg_ref, wu_ref, wo_ref = refs[:5]
    pos = 5
    side_in = refs[pos:pos + n_side]
    pos += n_side + (1 if has_prev else 0)
    o_ref = refs[pos]
    side_out = refs[pos + 1:pos + 1 + n_side]
    pos += 1 + n_side
    own16 = refs[pos:pos + 3] if emit_weights else ()
    h_ref = refs[-1]
    j = pl.program_id(1)

    @pl.when(j == 0)
    def _():
        x = x_ref[...]
        ms = jnp.mean(x * x, axis=-1, keepdims=True)
        h_ref[...] = (x * lax.rsqrt(ms + EPS) * g_ref[...]).astype(BF16)
        o_ref[...] = x

    for src, dst in zip(side_in, side_out):
        dst[...] = src[...].astype(BF16)

    weights = [w_ref[...].astype(BF16) for w_ref in (wg_ref, wu_ref, wo_ref)]
    for w16, dst in zip(weights, own16):
        dst[...] = w16
    wg, wu, wo = weights
    h = h_ref[...]
    gate = jnp.dot(h, wg, preferred_element_type=F32)
    up = jnp.dot(h, wu, preferred_element_type=F32)
    act = (gate * jax.nn.sigmoid(gate) * up * 0.5).astype(BF16)
    o_ref[...] += jnp.dot(act, wo, preferred_element_type=F32)


def _convert_rows(n_rows, n_steps):
    rows = BF16_SUBLANES
    while n_rows % rows or n_rows // rows > n_steps:
        rows += BF16_SUBLANES
    return rows


def _ffn(x2d, gain, wg, wu, wo, *, tm, tf, up_block_offset, first_tile, n_tiles,
         prev_out=None, convert=(), emit_weights=False):
    T, D = x2d.shape
    d_ff = wo.shape[0]
    nj = d_ff // tf
    n_steps = n_tiles * nj
    side_specs, side_shapes = [], []
    for w in convert:
        rows = _convert_rows(w.shape[0], n_steps)
        pieces = w.shape[0] // rows
        side_specs.append(pl.BlockSpec(
            (rows, w.shape[1]), lambda i, j, pieces=pieces: (jnp.minimum(i * nj + j, pieces - 1), 0)))
        side_shapes.append(jax.ShapeDtypeStruct(w.shape, BF16))
    own_specs, own_shapes = [], []
    if emit_weights:
        assert n_tiles == 1
        own_specs = [pl.BlockSpec((D, tf), lambda i, j: (0, j)), pl.BlockSpec((D, tf), lambda i, j: (0, j)),
                     pl.BlockSpec((tf, D), lambda i, j: (j, 0))]
        own_shapes = [jax.ShapeDtypeStruct((D, d_ff), BF16), jax.ShapeDtypeStruct((D, d_ff), BF16),
                      jax.ShapeDtypeStruct((d_ff, D), BF16)]
    prev_specs = [] if prev_out is None else [pl.BlockSpec(memory_space=pl.ANY)]
    prev_args = [] if prev_out is None else [prev_out]
    n_in = 5 + len(convert)
    outs = pl.pallas_call(
        functools.partial(_ffn_kernel, n_side=len(convert), has_prev=prev_out is not None,
                          emit_weights=emit_weights),
        grid=(n_tiles, nj),
        in_specs=[
            pl.BlockSpec((tm, D), lambda i, j: (i + first_tile, 0)),
            pl.BlockSpec((1, D), lambda i, j: (0, 0)),
            pl.BlockSpec((D, tf), lambda i, j: (0, j)),
            pl.BlockSpec((D, tf), lambda i, j: (0, j + up_block_offset)),
            pl.BlockSpec((tf, D), lambda i, j: (j, 0)),
        ] + side_specs + prev_specs,
        out_specs=[pl.BlockSpec((tm, D), lambda i, j: (i + first_tile, 0))] + side_specs + own_specs,
        out_shape=[jax.ShapeDtypeStruct((T, D), F32)] + side_shapes + own_shapes,
        input_output_aliases={} if prev_out is None else {n_in: 0},
        scratch_shapes=[pltpu.VMEM((tm, D), BF16)],
        compiler_params=_cparams(("parallel", "arbitrary")),
        name="ffn",
    )(x2d, gain.reshape(1, D), wg, wu, wo, *convert, *prev_args)
    n_side = len(convert)
    return outs[0], tuple(outs[1:1 + n_side]), tuple(outs[1 + n_side:])


def _ffn_split(x2d, gain, w_in, w_out, *, tm, convert=()):
    T = x2d.shape[0]
    d_ff = w_out.shape[0]
    tf_f32, tf_bf16 = 256, 512
    head, _, (wg16, wu16, wo16) = _ffn(
        x2d, gain, w_in, w_in, w_out, tm=tm, tf=tf_f32, up_block_offset=d_ff // tf_f32,
        first_tile=0, n_tiles=1, emit_weights=True)
    out, side, _ = _ffn(
        x2d, gain, wg16, wu16, wo16, tm=tm, tf=tf_bf16, up_block_offset=0,
        first_tile=1, n_tiles=T // tm - 1, prev_out=head, convert=convert)
    return out, side


def _rope_tables(seq, sub_dim):
    rd = sub_dim // ROPE_FRACTION
    half = rd // 2
    inv = ROPE_THETA ** (-np.arange(0, rd, 2, dtype=np.float64) / rd)
    ang = np.arange(seq, dtype=np.float64)[:, None] * inv[None, :]
    cos, sin = np.cos(ang), np.sin(ang)
    c = np.ones((seq, sub_dim))
    s = np.zeros((seq, sub_dim))
    c[:, :half] = cos
    c[:, half:rd] = cos
    s[:, :half] = -sin
    s[:, half:rd] = sin
    reps = HEAD_DIM // sub_dim
    return (np.tile(c, (1, reps)).astype(np.float32), np.tile(s, (1, reps)).astype(np.float32))


def _inproj_kernel(x_ref, g_ref, w_ref, cos_a_ref, sin_a_ref, cos_b_ref, sin_b_ref, gain_ref,
                   oa_ref, ob_ref):
    x = x_ref[...]
    ms = jnp.mean(x * x, axis=-1, keepdims=True)
    h = (x * lax.rsqrt(ms + EPS) * g_ref[...]).astype(BF16)

    lane = lax.broadcasted_iota(jnp.int32, (1, HEAD_DIM), 1)
    lo = lane < B_SUB_DIM
    pair = 2 * HEAD_DIM
    pairs_per_group = GROUP_WIDTH // pair

    def project(col):
        return jnp.dot(h, w_ref[:, col:col + pair], preferred_element_type=F32)

    def qk_epilogue(ph, sub_dim, gain, cos, sin):
        half = sub_dim // ROPE_FRACTION // 2
        sq = ph * ph
        if sub_dim == HEAD_DIM:
            inv = lax.rsqrt(jnp.sum(sq, axis=-1, keepdims=True) * (1.0 / sub_dim) + EPS)
        else:
            ms_lo = jnp.sum(jnp.where(lo, sq, 0.0), axis=-1, keepdims=True) * (1.0 / sub_dim)
            ms_hi = jnp.sum(jnp.where(lo, 0.0, sq), axis=-1, keepdims=True) * (1.0 / sub_dim)
            inv = jnp.where(lo, lax.rsqrt(ms_lo + EPS), lax.rsqrt(ms_hi + EPS))
        y = ph * inv * gain
        rot = jnp.where(lane % sub_dim < half, pltpu.roll(y, HEAD_DIM - half, 1), pltpu.roll(y, half, 1))
        return y * cos + rot * sin

    groups = [(0, oa_ref, 0, HEAD_DIM, 0), (1, oa_ref, 1, HEAD_DIM, 1),
              (3, ob_ref, 0, B_SUB_DIM, 2), (4, ob_ref, 1, B_SUB_DIM, 3),
              (2, oa_ref, 2, None, None), (5, ob_ref, 2, None, None)]
    steps = [(grp, pr) for grp in groups for pr in range(pairs_per_group)]
    col_of = lambda step: step[0][0] * GROUP_WIDTH + step[1] * pair
    p_next = project(col_of(steps[0]))
    for n, ((_, out_ref, slot, sub_dim, gain_row), pr) in enumerate(steps):
        p, p_next = p_next, (project(col_of(steps[n + 1])) if n + 1 < len(steps) else None)
        for e in range(2):
            ph = p[:, e * HEAD_DIM:(e + 1) * HEAD_DIM]
            if sub_dim == HEAD_DIM:
                ph = qk_epilogue(ph, sub_dim, gain_ref[gain_row], cos_a_ref[...], sin_a_ref[...])
            elif sub_dim == B_SUB_DIM:
                ph = qk_epilogue(ph, sub_dim, gain_ref[gain_row], cos_b_ref[...], sin_b_ref[...])
            out_ref[slot, 2 * pr + e] = ph.astype(out_ref.dtype)


def _inproj(x2d, mix_gain, w_in, a_q_gain, a_k_gain, b_q_gain, b_k_gain, *, batch, seq, tm=512):
    T, D = x2d.shape
    spb = seq // tm
    cos_a, sin_a = _rope_tables(seq, HEAD_DIM)
    cos_b, sin_b = _rope_tables(seq, B_SUB_DIM)
    reps = HEAD_DIM // B_SUB_DIM
    gains = jnp.stack([a_q_gain * (LOG2E * HEAD_DIM ** -0.5), a_k_gain,
                       jnp.tile(b_q_gain, reps) * (LOG2E * B_SUB_DIM ** -0.5),
                       jnp.tile(b_k_gain, reps)]).reshape(4, 1, HEAD_DIM)
    table = lambda: pl.BlockSpec((tm, HEAD_DIM), lambda i: (i % spb, 0))
    out = lambda: pl.BlockSpec((3, None, N_HEADS, tm, HEAD_DIM), lambda i: (0, i // spb, 0, i % spb, 0))
    return pl.pallas_call(
        _inproj_kernel,
        grid=(T // tm,),
        in_specs=[
            pl.BlockSpec((tm, D), lambda i: (i, 0)),
            pl.BlockSpec((1, D), lambda i: (0, 0)),
            pl.BlockSpec(w_in.shape, lambda i: (0, 0), pipeline_mode=pl.Buffered(1)),
            table(), table(), table(), table(),
            pl.BlockSpec((4, 1, HEAD_DIM), lambda i: (0, 0, 0)),
        ],
        out_specs=[out(), out()],
        out_shape=[jax.ShapeDtypeStruct((3, batch, N_HEADS, seq, HEAD_DIM), F32),
                   jax.ShapeDtypeStruct((3, batch, N_HEADS, seq, HEAD_DIM), BF16)],
        compiler_params=_cparams(("parallel",)),
        name="inproj",
    )(x2d, mix_gain.reshape(1, D), w_in, jnp.asarray(cos_a), jnp.asarray(sin_a),
      jnp.asarray(cos_b), jnp.asarray(sin_b), gains)


def _band_bias():
    col_minus_row = np.arange(K_BLK)[None, :] - np.arange(Q_BLK)[:, None]
    return np.stack([np.where(np.abs(col_minus_row - lead) <= HALF_WIN, 0.0, NEG)
                     for lead in (0, HALF_WIN, 2 * HALF_WIN)]).astype(np.float32)


def _dilated_kernel(q_ref, k_ref, v_ref, bias_ref, g_ref, o_ref,
                    qs_ref, ks_ref, vs_ref, mid_ref, og_ref, lg_ref, *, seq):
    stage_rows = 256
    srcs, dsts = (q_ref, k_ref, v_ref), (qs_ref, ks_ref, vs_ref)
    (_, dil0), (_, dil1), (_, dil2) = PATTERNS
    assert dil0 == 1 and dil1 == DIL_STEP and dil2 == DIL_STEP * dil1 and seq // dil2 == stage_rows
    sub1 = seq // dil1

    def stage0(t, carry):
        rows = pl.ds(pl.multiple_of(t * stage_rows, stage_rows), stage_rows)
        for src, dst in zip(srcs, dsts):
            dst[0, rows, :] = src[rows, :].astype(BF16)
        return carry

    def stage1(t, carry):
        per_res = sub1 // stage_rows
        r = t // per_res
        c0 = (t % per_res) * stage_rows
        rows = pl.ds(pl.multiple_of(r * sub1 + c0, stage_rows), stage_rows)
        for a, (src, dst) in enumerate(zip(srcs, dsts)):
            x = src[pl.ds(r + dil1 * c0, stage_rows, stride=dil1), :]
            mid_ref[a, rows, :] = x
            dst[1, rows, :] = x.astype(BF16)
        return carry

    def stage2(t, carry):
        r1 = t // DIL_STEP
        rr = t % DIL_STEP
        rows = pl.ds(pl.multiple_of((dil1 * rr + r1) * stage_rows, stage_rows), stage_rows)
        for a, dst in enumerate(dsts):
            dst[2, rows, :] = mid_ref[a, pl.ds(r1 * sub1 + rr, stage_rows, stride=DIL_STEP), :].astype(BF16)
        return carry

    for stage in (stage0, stage1, stage2):
        lax.fori_loop(0, seq // stage_rows, stage, 0)

    unroll = 16
    for g, (_, dil) in enumerate(PATTERNS):
        sub_len = seq // dil
        nblk = sub_len // Q_BLK

        def body(it, carry, g=g, dil=dil, sub_len=sub_len, nblk=nblk):
            kvs, outs, scores = [], [], []
            for u in range(unroll):
                t = it * unroll + u
                r = t // nblk
                m0 = (t % nblk) * Q_BLK
                k0 = jnp.clip(m0 - HALF_WIN, 0, sub_len - K_BLK)
                base = r * sub_len
                q = qs_ref[g, pl.ds(pl.multiple_of(base + m0, Q_BLK), Q_BLK), :]
                kv = pl.ds(pl.multiple_of(base + k0, HALF_WIN), K_BLK)
                s = lax.dot_general(q, ks_ref[g, kv, :], (((1,), (1,)), ((), ())),
                                    preferred_element_type=F32)
                scores.append(s + bias_ref[(m0 - k0) // HALF_WIN])
                kvs.append(kv)
                if dil == 1:
                    outs.append(pl.ds(pl.multiple_of(m0, Q_BLK), Q_BLK))
                else:
                    outs.append(pl.ds(r + dil * m0, Q_BLK, stride=dil))
            s = jnp.concatenate(scores, axis=0)
            m = jnp.max(s, axis=-1, keepdims=True)
            p = jnp.exp2(s - m)
            l = jnp.sum(p, axis=-1, keepdims=True)
            p = p.astype(BF16)
            inv_l = 1.0 / l
            lse = jnp.broadcast_to(m + jnp.log2(l), (unroll * Q_BLK, HEAD_DIM))
            for u in range(unroll):
                blk = slice(u * Q_BLK, (u + 1) * Q_BLK)
                o = jnp.dot(p[blk], vs_ref[g, kvs[u], :], preferred_element_type=F32)
                og_ref[g, outs[u], :] = o * inv_l[blk]
                lg_ref[g, outs[u], :] = lse[blk]
            return carry

        lax.fori_loop(0, dil * nblk // unroll, body, 0)

    chunk = 1024

    def comb(c, carry):
        rows = pl.ds(pl.multiple_of(c * chunk, chunk), chunk)
        l0, l1, l2 = lg_ref[0, rows, :], lg_ref[1, rows, :], lg_ref[2, rows, :]
        m = jnp.maximum(jnp.maximum(l0, l1), l2)
        w0, w1, w2 = jnp.exp2(l0 - m), jnp.exp2(l1 - m), jnp.exp2(l2 - m)
        o = (w0 * og_ref[0, rows, :] + w1 * og_ref[1, rows, :] + w2 * og_ref[2, rows, :]) / (w0 + w1 + w2)
        ms = jnp.mean(o * o, axis=-1, keepdims=True)
        o_ref[rows, :] = (o * lax.rsqrt(ms + EPS) * g_ref[...]).astype(o_ref.dtype)
        return carry

    lax.fori_loop(0, seq // chunk, comb, 0)


def _dilated(qkv, out_gain):
    _, batch, nh, seq, hd = qkv.shape
    npat = len(PATTERNS)
    spec = lambda which: pl.BlockSpec((None, None, None, seq, hd), lambda b, h: (which, b, h, 0, 0))
    return pl.pallas_call(
        functools.partial(_dilated_kernel, seq=seq),
        grid=(batch, nh),
        in_specs=[spec(0), spec(1), spec(2),
                  pl.BlockSpec((3, Q_BLK, K_BLK), lambda b, h: (0, 0, 0)),
                  pl.BlockSpec((1, hd), lambda b, h: (0, 0))],
        out_specs=pl.BlockSpec((None, seq, hd), lambda b, h: (b, 0, h)),
        out_shape=jax.ShapeDtypeStruct((batch, seq, nh * hd), BF16),
        scratch_shapes=[pltpu.VMEM((npat, seq, hd), BF16),
                        pltpu.VMEM((npat, seq, hd), BF16),
                        pltpu.VMEM((npat, seq, hd), BF16),
                        pltpu.VMEM((3, seq, hd), F32),
                        pltpu.VMEM((npat, seq, hd), F32),
                        pltpu.VMEM((npat, seq, hd), F32)],
        compiler_params=_cparams(("parallel", "parallel")),
        name="dilated",
    )(qkv, qkv, qkv, jnp.asarray(_band_bias()), out_gain.reshape(1, hd))


def _diff_kernel(lam_ref, q_ref, k_ref, v_ref, g_ref, o_ref, vaug_ref, s0_ref, *,
                 out_scale, lambda_init, tq, kc):
    seq = k_ref.shape[0]
    vaug_ref[:, :HEAD_DIM] = v_ref[...]
    vaug_ref[:, HEAD_DIM:] = jnp.ones((seq, HEAD_DIM), BF16)

    lp = lam_ref[...]
    lam = (jnp.exp(jnp.sum(lp[0:1] * lp[1:2], axis=-1, keepdims=True))
           - jnp.exp(jnp.sum(lp[2:3] * lp[3:4], axis=-1, keepdims=True)) + lambda_init)
    gain = g_ref[...] * out_scale
    lane = lax.broadcasted_iota(jnp.int32, (1, HEAD_DIM), 1)
    nc = seq // kc
    n_tiles = seq // tq

    def stacked_q(i):
        q = q_ref[pl.ds(pl.multiple_of(i * tq, tq), tq), :]
        zero = jnp.zeros_like(q)
        return jnp.concatenate([jnp.where(lane < B_SUB_DIM, q, zero),
                                jnp.where(lane < B_SUB_DIM, zero, q)], axis=0)

    def scores(q_st, c):
        return lax.dot_general(q_st, k_ref[c * kc:(c + 1) * kc, :], (((1,), (1,)), ((), ())),
                               preferred_element_type=F32)

    s0_ref[...] = scores(stacked_q(0), 0)

    def tile(i, carry):
        rows = pl.ds(pl.multiple_of(i * tq, tq), tq)
        q_st = stacked_q(i)
        m = acc = None
        s_next = s0_ref[...]
        for c in range(nc):
            s = s_next
            if c + 1 < nc:
                s_next = scores(q_st, c + 1)
            else:
                s0_ref[...] = scores(stacked_q(jnp.minimum(i + 1, n_tiles - 1)), 0)
            m_c = jnp.max(s, axis=-1, keepdims=True)
            m_new = m_c if m is None else jnp.maximum(m, m_c)
            pv = jnp.dot(jnp.exp2(s - m_new).astype(BF16), vaug_ref[c * kc:(c + 1) * kc, :],
                         preferred_element_type=F32)
            acc = pv if m is None else acc * jnp.exp2(m - m_new) + pv
            m = m_new
        o_st = acc[:, :HEAD_DIM] / acc[:, HEAD_DIM:]
        o = o_st[:tq] - lam * o_st[tq:]
        ms = jnp.mean(o * o, axis=-1, keepdims=True)
        o_ref[rows, :] = (o * lax.rsqrt(ms + EPS) * gain).astype(o_ref.dtype)
        return carry

    lax.fori_loop(0, n_tiles, tile, 0)


def _diff(qkv, lam_params, out_gain, *, lambda_init, tq=512, kc=1024):
    _, batch, nh, seq, hd = qkv.shape
    kern = functools.partial(_diff_kernel, out_scale=1.0 - lambda_init, lambda_init=lambda_init,
                             tq=tq, kc=kc)
    spec = lambda which: pl.BlockSpec((None, None, None, seq, hd), lambda b, h: (which, b, h, 0, 0))
    return pl.pallas_call(
        kern,
        grid=(batch, nh),
        in_specs=[pl.BlockSpec((4, B_SUB_DIM), lambda b, h: (0, 0)),
                  spec(0), spec(1), spec(2),
                  pl.BlockSpec((1, hd), lambda b, h: (0, 0))],
        out_specs=pl.BlockSpec((None, seq, hd), lambda b, h: (b, 0, h)),
        out_shape=jax.ShapeDtypeStruct((batch, seq, nh * hd), BF16),
        scratch_shapes=[pltpu.VMEM((seq, 2 * hd), BF16), pltpu.VMEM((2 * tq, kc), F32)],
        compiler_params=_cparams(("parallel", "parallel")),
        name="diff",
    )(lam_params, qkv, qkv, qkv, out_gain.reshape(1, hd))


def _outproj_kernel(x_ref, a_ref, b_ref, wa_ref, wb_ref, o_ref):
    o_ref[...] = (x_ref[...]
                  + jnp.dot(a_ref[...], wa_ref[...], preferred_element_type=F32)
                  + jnp.dot(b_ref[...], wb_ref[...], preferred_element_type=F32))


def _outproj(x2d, a2d, b2d, w_out, *, tm=512):
    T, D = x2d.shape
    W = a2d.shape[1]
    return pl.pallas_call(
        _outproj_kernel,
        grid=(T // tm,),
        in_specs=[
            pl.BlockSpec((tm, D), lambda i: (i, 0)),
            pl.BlockSpec((tm, W), lambda i: (i, 0)),
            pl.BlockSpec((tm, W), lambda i: (i, 0)),
            pl.BlockSpec((W, D), lambda i: (0, 0)),
            pl.BlockSpec((W, D), lambda i: (1, 0)),
        ],
        out_specs=pl.BlockSpec((tm, D), lambda i: (i, 0)),
        out_shape=jax.ShapeDtypeStruct((T, D), F32),
        compiler_params=_cparams(("parallel",)),
        name="outproj",
    )(x2d, a2d, b2d, w_out, w_out)


def _layer(x, layer_idx, ffn1_norm, ffn1_w_in, ffn1_w_out, mix_norm, w_in,
           a_q_norm, a_k_norm, b_q_norm, b_k_norm,
           lambda_q1, lambda_k1, lambda_q2, lambda_k2,
           a_out_norm, b_out_norm, w_out, ffn2_norm, ffn2_w_in, ffn2_w_out):
    batch, seq, d_model = x.shape
    x2d = x.reshape(batch * seq, d_model)
    lambda_init = 0.8 - 0.6 * math.exp(-0.3 * layer_idx)

    x1, (w_in16, w_out16) = _ffn_split(x2d, ffn1_norm, ffn1_w_in, ffn1_w_out, tm=1024,
                                       convert=(w_in, w_out))

    qkv_a, qkv_b = _inproj(x1, mix_norm, w_in16, a_q_norm, a_k_norm, b_q_norm, b_k_norm,
                           batch=batch, seq=seq)

    a_o = _dilated(qkv_a, a_out_norm)
    lam_params = jnp.stack([lambda_q1, lambda_k1, lambda_q2, lambda_k2])
    b_o = _diff(qkv_b, lam_params, b_out_norm, lambda_init=lambda_init)

    x2 = _outproj(x1, a_o.reshape(batch * seq, GROUP_WIDTH), b_o.reshape(batch * seq, GROUP_WIDTH),
                  w_out16)
    out, _ = _ffn_split(x2, ffn2_norm, ffn2_w_in, ffn2_w_out, tm=1024)
    return out.reshape(batch, seq, d_model)


def kernel(x, ffn1_norm, ffn1_w_in, ffn1_w_out, mix_norm, w_in, a_q_norm, a_k_norm, b_q_norm, b_k_norm,
           lambda_q1, lambda_k1, lambda_q2, lambda_k2, a_out_norm, b_out_norm, w_out,
           ffn2_norm, ffn2_w_in, ffn2_w_out):
    for l in range(ffn1_norm.shape[0]):
        x = _layer(x, l, ffn1_norm[l], ffn1_w_in[l], ffn1_w_out[l], mix_norm[l], w_in[l],
                   a_q_norm[l], a_k_norm[l], b_q_norm[l], b_k_norm[l],
                   lambda_q1[l], lambda_k1[l], lambda_q2[l], lambda_k2[l],
                   a_out_norm[l], b_out_norm[l], w_out[l],
                   ffn2_norm[l], ffn2_w_in[l], ffn2_w_out[l])
    return x
```

```python
import functools
import math

import numpy as np
import jax
import jax.numpy as jnp
from jax import lax
from jax.experimental import pallas as pl
from jax.experimental.pallas import tpu as pltpu

F32 = jnp.float32
BF16 = jnp.bfloat16

HEAD_DIM = 128
N_HEADS = 8
GROUP_WIDTH = N_HEADS * HEAD_DIM
B_SUB_DIM = 64
ROPE_THETA = 500000.0
ROPE_FRACTION = 4
PATTERNS = ((128, 1), (512, 4), (2048, 16))
DIL_STEP = 4
HALF_WIN = 64
EPS = 1e-6
NEG = -1e30
LOG2E = math.log2(math.e)

Q_BLK = 128
K_BLK = Q_BLK + 2 * HALF_WIN

TILES_PER_ITER = 4
BF16_SUBLANES = 16
VMEM_LIMIT = 60 * 1024 * 1024


def _cparams(sem):
    return pltpu.CompilerParams(dimension_semantics=sem, vmem_limit_bytes=VMEM_LIMIT)


def _ffn_kernel(*refs, n_side, has_prev, emit_weights):
    x_ref, g_ref, wg_ref, wu_ref, wo_ref = refs[:5]
    pos = 5
    side_in = refs[pos:pos + n_side]
    pos += n_side + (1 if has_prev else 0)
    o_ref = refs[pos]
    side_out = refs[pos + 1:pos + 1 + n_side]
    pos += 1 + n_side
    own16 = refs[pos:pos + 3] if emit_weights else ()
    h_ref = refs[-1]
    j = pl.program_id(1)

    @pl.when(j == 0)
    def _():
        x = x_ref[...]
        ms = jnp.mean(x * x, axis=-1, keepdims=True)
        h_ref[...] = (x * lax.rsqrt(ms + EPS) * g_ref[...]).astype(BF16)
        o_ref[...] = x

    for src, dst in zip(side_in, side_out):
        dst[...] = src[...].astype(BF16)

    weights = [w_ref[...].astype(BF16) for w_ref in (wg_ref, wu_ref, wo_ref)]
    for w16, dst in zip(weights, own16):
        dst[...] = w16
    wg, wu, wo = weights
    h = h_ref[...]
    gate = jnp.dot(h, wg, preferred_element_type=F32)
    up = jnp.dot(h, wu, preferred_element_type=F32)
    act = (gate * jax.nn.sigmoid(gate) * up * 0.5).astype(BF16)
    o_ref[...] += jnp.dot(act, wo, preferred_element_type=F32)


def _convert_rows(n_rows, n_steps):
    rows = BF16_SUBLANES
    while n_rows % rows or n_rows // rows > n_steps:
        rows += BF16_SUBLANES
    return rows


def _ffn(x2d, gain, wg, wu, wo, *, tm, tf, up_block_offset, first_tile, n_tiles,
         prev_out=None, convert=(), emit_weights=False):
    T, D = x2d.shape
    d_ff = wo.shape[0]
    nj = d_ff // tf
    n_steps = n_tiles * nj
    side_specs, side_shapes = [], []
    for w in convert:
        rows = _convert_rows(w.shape[0], n_steps)
        pieces = w.shape[0] // rows
        side_specs.append(pl.BlockSpec(
            (rows, w.shape[1]), lambda i, j, pieces=pieces: (jnp.minimum(i * nj + j, pieces - 1), 0)))
        side_shapes.append(jax.ShapeDtypeStruct(w.shape, BF16))
    own_specs, own_shapes = [], []
    if emit_weights:
        assert n_tiles == 1
        own_specs = [pl.BlockSpec((D, tf), lambda i, j: (0, j)), pl.BlockSpec((D, tf), lambda i, j: (0, j)),
                     pl.BlockSpec((tf, D), lambda i, j: (j, 0))]
        own_shapes = [jax.ShapeDtypeStruct((D, d_ff), BF16), jax.ShapeDtypeStruct((D, d_ff), BF16),
                      jax.ShapeDtypeStruct((d_ff, D), BF16)]
    prev_specs = [] if prev_out is None else [pl.BlockSpec(memory_space=pl.ANY)]
    prev_args = [] if prev_out is None else [prev_out]
    n_in = 5 + len(convert)
    outs = pl.pallas_call(
        functools.partial(_ffn_kernel, n_side=len(convert), has_prev=prev_out is not None,
                          emit_weights=emit_weights),
        grid=(n_tiles, nj),
        in_specs=[
            pl.BlockSpec((tm, D), lambda i, j: (i + first_tile, 0)),
            pl.BlockSpec((1, D), lambda i, j: (0, 0)),
            pl.BlockSpec((D, tf), lambda i, j: (0, j)),
            pl.BlockSpec((D, tf), lambda i, j: (0, j + up_block_offset)),
            pl.BlockSpec((tf, D), lambda i, j: (j, 0)),
        ] + side_specs + prev_specs,
        out_specs=[pl.BlockSpec((tm, D), lambda i, j: (i + first_tile, 0))] + side_specs + own_specs,
        out_shape=[jax.ShapeDtypeStruct((T, D), F32)] + side_shapes + own_shapes,
        input_output_aliases={} if prev_out is None else {n_in: 0},
        scratch_shapes=[pltpu.VMEM((tm, D), BF16)],
        compiler_params=_cparams(("parallel", "arbitrary")),
        name="ffn",
    )(x2d, gain.reshape(1, D), wg, wu, wo, *convert, *prev_args)
    n_side = len(convert)
    return outs[0], tuple(outs[1:1 + n_side]), tuple(outs[1 + n_side:])


def _ffn_split(x2d, gain, w_in, w_out, *, tm, convert=()):
    T = x2d.shape[0]
    d_ff = w_out.shape[0]
    tf_f32, tf_bf16 = 256, 512
    head, _, (wg16, wu16, wo16) = _ffn(
        x2d, gain, w_in, w_in, w_out, tm=tm, tf=tf_f32, up_block_offset=d_ff // tf_f32,
        first_tile=0, n_tiles=1, emit_weights=True)
    out, side, _ = _ffn(
        x2d, gain, wg16, wu16, wo16, tm=tm, tf=tf_bf16, up_block_offset=0,
        first_tile=1, n_tiles=T // tm - 1, prev_out=head, convert=convert)
    return out, side


def _rope_tables(seq, sub_dim):
    rd = sub_dim // ROPE_FRACTION
    half = rd // 2
    inv = ROPE_THETA ** (-np.arange(0, rd, 2, dtype=np.float64) / rd)
    ang = np.arange(seq, dtype=np.float64)[:, None] * inv[None, :]
    cos, sin = np.cos(ang), np.sin(ang)
    c = np.ones((seq, sub_dim))
    s = np.zeros((seq, sub_dim))
    c[:, :half] = cos
    c[:, half:rd] = cos
    s[:, :half] = -sin
    s[:, half:rd] = sin
    reps = HEAD_DIM // sub_dim
    return (np.tile(c, (1, reps)).astype(np.float32), np.tile(s, (1, reps)).astype(np.float32))


def _inproj_kernel(x_ref, g_ref, w_ref, cos_a_ref, sin_a_ref, cos_b_ref, sin_b_ref, gain_ref,
                   oa_ref, ob_ref):
    x = x_ref[...]
    ms = jnp.mean(x * x, axis=-1, keepdims=True)
    h = (x * lax.rsqrt(ms + EPS) * g_ref[...]).astype(BF16)

    lane = lax.broadcasted_iota(jnp.int32, (1, HEAD_DIM), 1)
    lo = lane < B_SUB_DIM
    pair = 2 * HEAD_DIM
    pairs_per_group = GROUP_WIDTH // pair

    def project(col):
        return jnp.dot(h, w_ref[:, col:col + pair], preferred_element_type=F32)

    def qk_epilogue(ph, sub_dim, gain, cos, sin):
        half = sub_dim // ROPE_FRACTION // 2
        sq = ph * ph
        if sub_dim == HEAD_DIM:
            inv = lax.rsqrt(jnp.sum(sq, axis=-1, keepdims=True) * (1.0 / sub_dim) + EPS)
        else:
            ms_lo = jnp.sum(jnp.where(lo, sq, 0.0), axis=-1, keepdims=True) * (1.0 / sub_dim)
            ms_hi = jnp.sum(jnp.where(lo, 0.0, sq), axis=-1, keepdims=True) * (1.0 / sub_dim)
            inv = jnp.where(lo, lax.rsqrt(ms_lo + EPS), lax.rsqrt(ms_hi + EPS))
        y = ph * inv * gain
        rot = jnp.where(lane % sub_dim < half, pltpu.roll(y, HEAD_DIM - half, 1), pltpu.roll(y, half, 1))
        return y * cos + rot * sin

    groups = [(0, oa_ref, 0, HEAD_DIM, 0), (1, oa_ref, 1, HEAD_DIM, 1),
              (3, ob_ref, 0, B_SUB_DIM, 2), (4, ob_ref, 1, B_SUB_DIM, 3),
              (2, oa_ref, 2, None, None), (5, ob_ref, 2, None, None)]
    steps = [(grp, pr) for grp in groups for pr in range(pairs_per_group)]
    col_of = lambda step: step[0][0] * GROUP_WIDTH + step[1] * pair
    p_next = project(col_of(steps[0]))
    for n, ((_, out_ref, slot, sub_dim, gain_row), pr) in enumerate(steps):
        p, p_next = p_next, (project(col_of(steps[n + 1])) if n + 1 < len(steps) else None)
        for e in range(2):
            ph = p[:, e * HEAD_DIM:(e + 1) * HEAD_DIM]
            if sub_dim == HEAD_DIM:
                ph = qk_epilogue(ph, sub_dim, gain_ref[gain_row], cos_a_ref[...], sin_a_ref[...])
            elif sub_dim == B_SUB_DIM:
                ph = qk_epilogue(ph, sub_dim, gain_ref[gain_row], cos_b_ref[...], sin_b_ref[...])
            out_ref[slot, 2 * pr + e] = ph.astype(out_ref.dtype)


def _inproj(x2d, mix_gain, w_in, a_q_gain, a_k_gain, b_q_gain, b_k_gain, *, batch, seq, tm=512):
    T, D = x2d.shape
    spb = seq // tm
    cos_a, sin_a = _rope_tables(seq, HEAD_DIM)
    cos_b, sin_b = _rope_tables(seq, B_SUB_DIM)
    reps = HEAD_DIM // B_SUB_DIM
    gains = jnp.stack([a_q_gain * (LOG2E * HEAD_DIM ** -0.5), a_k_gain,
                       jnp.tile(b_q_gain, reps) * (LOG2E * B_SUB_DIM ** -0.5),
                       jnp.tile(b_k_gain, reps)]).reshape(4, 1, HEAD_DIM)
    table = lambda: pl.BlockSpec((tm, HEAD_DIM), lambda i: (i % spb, 0))
    out = lambda: pl.BlockSpec((3, None, N_HEADS, tm, HEAD_DIM), lambda i: (0, i // spb, 0, i % spb, 0))
    return pl.pallas_call(
        _inproj_kernel,
        grid=(T // tm,),
        in_specs=[
            pl.BlockSpec((tm, D), lambda i: (i, 0)),
            pl.BlockSpec((1, D), lambda i: (0, 0)),
            pl.BlockSpec(w_in.shape, lambda i: (0, 0), pipeline_mode=pl.Buffered(1)),
            table(), table(), table(), table(),
            pl.BlockSpec((4, 1, HEAD_DIM), lambda i: (0, 0, 0)),
        ],
        out_specs=[out(), out()],
        out_shape=[jax.ShapeDtypeStruct((3, batch, N_HEADS, seq, HEAD_DIM), F32),
                   jax.ShapeDtypeStruct((3, batch, N_HEADS, seq, HEAD_DIM), BF16)],
        compiler_params=_cparams(("parallel",)),
        name="inproj",
    )(x2d, mix_gain.reshape(1, D), w_in, jnp.asarray(cos_a), jnp.asarray(sin_a),
      jnp.asarray(cos_b), jnp.asarray(sin_b), gains)


def _band_bias():
    col_minus_row = np.arange(K_BLK)[None, :] - np.arange(Q_BLK)[:, None]
    return np.stack([np.where(np.abs(col_minus_row - lead) <= HALF_WIN, 0.0, NEG)
                     for lead in (0, HALF_WIN, 2 * HALF_WIN)]).astype(np.float32)


def _dilated_kernel(q_ref, k_ref, v_ref, bias_ref, g_ref, o_ref,
                    qs_ref, ks_ref, vs_ref, mid_ref, og_ref, lg_ref, *, seq):
    stage_rows = 256
    srcs, dsts = (q_ref, k_ref, v_ref), (qs_ref, ks_ref, vs_ref)
    (_, dil0), (_, dil1), (_, dil2) = PATTERNS
    assert dil0 == 1 and dil1 == DIL_STEP and dil2 == DIL_STEP * dil1 and seq // dil2 == stage_rows
    sub1 = seq // dil1

    def stage0(t, carry):
        rows = pl.ds(pl.multiple_of(t * stage_rows, stage_rows), stage_rows)
        for src, dst in zip(srcs, dsts):
            dst[0, rows, :] = src[rows, :].astype(BF16)
        return carry

    def stage1(t, carry):
        per_res = sub1 // stage_rows
        r = t // per_res
        c0 = (t % per_res) * stage_rows
        rows = pl.ds(pl.multiple_of(r * sub1 + c0, stage_rows), stage_rows)
        for a, (src, dst) in enumerate(zip(srcs, dsts)):
            x = src[pl.ds(r + dil1 * c0, stage_rows, stride=dil1), :]
            mid_ref[a, rows, :] = x
            dst[1, rows, :] = x.astype(BF16)
        return carry

    def stage2(t, carry):
        r1 = t // DIL_STEP
        rr = t % DIL_STEP
        rows = pl.ds(pl.multiple_of((dil1 * rr + r1) * stage_rows, stage_rows), stage_rows)
        for a, dst in enumerate(dsts):
            dst[2, rows, :] = mid_ref[a, pl.ds(r1 * sub1 + rr, stage_rows, stride=DIL_STEP), :].astype(BF16)
        return carry

    for stage in (stage0, stage1, stage2):
        lax.fori_loop(0, seq // stage_rows, stage, 0)

    unroll = 16
    for g, (_, dil) in enumerate(PATTERNS):
        sub_len = seq // dil
        nblk = sub_len // Q_BLK

        def body(it, carry, g=g, dil=dil, sub_len=sub_len, nblk=nblk):
            kvs, outs, scores = [], [], []
            for u in range(unroll):
                t = it * unroll + u
                r = t // nblk
                m0 = (t % nblk) * Q_BLK
                k0 = jnp.clip(m0 - HALF_WIN, 0, sub_len - K_BLK)
                base = r * sub_len
                q = qs_ref[g, pl.ds(pl.multiple_of(base + m0, Q_BLK), Q_BLK), :]
                kv = pl.ds(pl.multiple_of(base + k0, HALF_WIN), K_BLK)
                s = lax.dot_general(q, ks_ref[g, kv, :], (((1,), (1,)), ((), ())),
                                    preferred_element_type=F32)
                scores.append(s + bias_ref[(m0 - k0) // HALF_WIN])
                kvs.append(kv)
                if dil == 1:
                    outs.append(pl.ds(pl.multiple_of(m0, Q_BLK), Q_BLK))
                else:
                    outs.append(pl.ds(r + dil * m0, Q_BLK, stride=dil))
            s = jnp.concatenate(scores, axis=0)
            m = jnp.max(s, axis=-1, keepdims=True)
            p = jnp.exp2(s - m)
            l = jnp.sum(p, axis=-1, keepdims=True)
            p = p.astype(BF16)
            inv_l = 1.0 / l
            lse = jnp.broadcast_to(m + jnp.log2(l), (unroll * Q_BLK, HEAD_DIM))
            for u in range(unroll):
                blk = slice(u * Q_BLK, (u + 1) * Q_BLK)
                o = jnp.dot(p[blk], vs_ref[g, kvs[u], :], preferred_element_type=F32)
                og_ref[g, outs[u], :] = o * inv_l[blk]
                lg_ref[g, outs[u], :] = lse[blk]
            return carry

        lax.fori_loop(0, dil * nblk // unroll, body, 0)

    chunk = 1024

    def comb(c, carry):
        rows = pl.ds(pl.multiple_of(c * chunk, chunk), chunk)
        l0, l1, l2 = lg_ref[0, rows, :], lg_ref[1, rows, :], lg_ref[2, rows, :]
        m = jnp.maximum(jnp.maximum(l0, l1), l2)
        w0, w1, w2 = jnp.exp2(l0 - m), jnp.exp2(l1 - m), jnp.exp2(l2 - m)
        o = (w0 * og_ref[0, rows, :] + w1 * og_ref[1, rows, :] + w2 * og_ref[2, rows, :]) / (w0 + w1 + w2)
        ms = jnp.mean(o * o, axis=-1, keepdims=True)
        o_ref[rows, :] = (o * lax.rsqrt(ms + EPS) * g_ref[...]).astype(o_ref.dtype)
        return carry

    lax.fori_loop(0, seq // chunk, comb, 0)


def _dilated(qkv, out_gain):
    _, batch, nh, seq, hd = qkv.shape
    npat = len(PATTERNS)
    spec = lambda which: pl.BlockSpec((None, None, None, seq, hd), lambda b, h: (which, b, h, 0, 0))
    return pl.pallas_call(
        functools.partial(_dilated_kernel, seq=seq),
        grid=(batch, nh),
        in_specs=[spec(0), spec(1), spec(2),
                  pl.BlockSpec((3, Q_BLK, K_BLK), lambda b, h: (0, 0, 0)),
                  pl.BlockSpec((1, hd), lambda b, h: (0, 0))],
        out_specs=pl.BlockSpec((None, seq, hd), lambda b, h: (b, 0, h)),
        out_shape=jax.ShapeDtypeStruct((batch, seq, nh * hd), BF16),
        scratch_shapes=[pltpu.VMEM((npat, seq, hd), BF16),
                        pltpu.VMEM((npat, seq, hd), BF16),
                        pltpu.VMEM((npat, seq, hd), BF16),
                        pltpu.VMEM((3, seq, hd), F32),
                        pltpu.VMEM((npat, seq, hd), F32),
                        pltpu.VMEM((npat, seq, hd), F32)],
        compiler_params=_cparams(("parallel", "parallel")),
        name="dilated",
    )(qkv, qkv, qkv, jnp.asarray(_band_bias()), out_gain.reshape(1, hd))


def _diff_kernel(lam_ref, q_ref, k_ref, v_ref, g_ref, o_ref, vaug_ref, s0_ref, *,
                 out_scale, lambda_init, tq, kc):
    seq = k_ref.shape[0]
    vaug_ref[:, :HEAD_DIM] = v_ref[...]
    vaug_ref[:, HEAD_DIM:] = jnp.ones((seq, HEAD_DIM), BF16)

    lp = lam_ref[...]
    lam = (jnp.exp(jnp.sum(lp[0:1] * lp[1:2], axis=-1, keepdims=True))
           - jnp.exp(jnp.sum(lp[2:3] * lp[3:4], axis=-1, keepdims=True)) + lambda_init)
    gain = g_ref[...] * out_scale
    lane = lax.broadcasted_iota(jnp.int32, (1, HEAD_DIM), 1)
    nc = seq // kc
    n_tiles = seq // tq

    def stacked_q(i):
        q = q_ref[pl.ds(pl.multiple_of(i * tq, tq), tq), :]
        zero = jnp.zeros_like(q)
        return jnp.concatenate([jnp.where(lane < B_SUB_DIM, q, zero),
                                jnp.where(lane < B_SUB_DIM, zero, q)], axis=0)

    def scores(q_st, c):
        return lax.dot_general(q_st, k_ref[c * kc:(c + 1) * kc, :], (((1,), (1,)), ((), ())),
                               preferred_element_type=F32)

    assert n_tiles % TILES_PER_ITER == 0
    s0_ref[...] = scores(stacked_q(0), 0)

    def one_tile(i, s_first, next_first):
        rows = pl.ds(pl.multiple_of(i * tq, tq), tq)
        q_st = stacked_q(i)
        m = acc = None
        s_next = s_first
        for c in range(nc):
            s = s_next
            s_next = scores(q_st, c + 1) if c + 1 < nc else next_first()
            m_c = jnp.max(s, axis=-1, keepdims=True)
            m_new = m_c if m is None else jnp.maximum(m, m_c)
            pv = jnp.dot(jnp.exp2(s - m_new).astype(BF16), vaug_ref[c * kc:(c + 1) * kc, :],
                         preferred_element_type=F32)
            acc = pv if m is None else acc * jnp.exp2(m - m_new) + pv
            m = m_new
        o_st = acc[:, :HEAD_DIM] / acc[:, HEAD_DIM:]
        o = o_st[:tq] - lam * o_st[tq:]
        ms = jnp.mean(o * o, axis=-1, keepdims=True)
        o_ref[rows, :] = (o * lax.rsqrt(ms + EPS) * gain).astype(o_ref.dtype)
        return s_next

    def tile_group(ii, carry):
        s_first = s0_ref[...]
        for t in range(TILES_PER_ITER):
            i = ii * TILES_PER_ITER + t
            if t + 1 < TILES_PER_ITER:
                s_first = one_tile(i, s_first, lambda i=i: scores(stacked_q(i + 1), 0))
            else:
                def hand_over(i=i):
                    s0_ref[...] = scores(stacked_q(jnp.minimum(i + 1, n_tiles - 1)), 0)
                one_tile(i, s_first, hand_over)
        return carry

    lax.fori_loop(0, n_tiles // TILES_PER_ITER, tile_group, 0)


def _diff(qkv, lam_params, out_gain, *, lambda_init, tq=512, kc=1024):
    _, batch, nh, seq, hd = qkv.shape
    kern = functools.partial(_diff_kernel, out_scale=1.0 - lambda_init, lambda_init=lambda_init,
                             tq=tq, kc=kc)
    spec = lambda which: pl.BlockSpec((None, None, None, seq, hd), lambda b, h: (which, b, h, 0, 0))
    return pl.pallas_call(
        kern,
        grid=(batch, nh),
        in_specs=[pl.BlockSpec((4, B_SUB_DIM), lambda b, h: (0, 0)),
                  spec(0), spec(1), spec(2),
                  pl.BlockSpec((1, hd), lambda b, h: (0, 0))],
        out_specs=pl.BlockSpec((None, seq, hd), lambda b, h: (b, 0, h)),
        out_shape=jax.ShapeDtypeStruct((batch, seq, nh * hd), BF16),
        scratch_shapes=[pltpu.VMEM((seq, 2 * hd), BF16), pltpu.VMEM((2 * tq, kc), F32)],
        compiler_params=_cparams(("parallel", "parallel")),
        name="diff",
    )(lam_params, qkv, qkv, qkv, out_gain.reshape(1, hd))


def _outproj_kernel(x_ref, a_ref, b_ref, wa_ref, wb_ref, o_ref):
    o_ref[...] = (x_ref[...]
                  + jnp.dot(a_ref[...], wa_ref[...], preferred_element_type=F32)
                  + jnp.dot(b_ref[...], wb_ref[...], preferred_element_type=F32))


def _outproj(x2d, a2d, b2d, w_out, *, tm=512):
    T, D = x2d.shape
    W = a2d.shape[1]
    return pl.pallas_call(
        _outproj_kernel,
        grid=(T // tm,),
        in_specs=[
            pl.BlockSpec((tm, D), lambda i: (i, 0)),
            pl.BlockSpec((tm, W), lambda i: (i, 0)),
            pl.BlockSpec((tm, W), lambda i: (i, 0)),
            pl.BlockSpec((W, D), lambda i: (0, 0)),
            pl.BlockSpec((W, D), lambda i: (1, 0)),
        ],
        out_specs=pl.BlockSpec((tm, D), lambda i: (i, 0)),
        out_shape=jax.ShapeDtypeStruct((T, D), F32),
        compiler_params=_cparams(("parallel",)),
        name="outproj",
    )(x2d, a2d, b2d, w_out, w_out)


def _layer(x, layer_idx, ffn1_norm, ffn1_w_in, ffn1_w_out, mix_norm, w_in,
           a_q_norm, a_k_norm, b_q_norm, b_k_norm,
           lambda_q1, lambda_k1, lambda_q2, lambda_k2,
           a_out_norm, b_out_norm, w_out, ffn2_norm, ffn2_w_in, ffn2_w_out):
    batch, seq, d_model = x.shape
    x2d = x.reshape(batch * seq, d_model)
    lambda_init = 0.8 - 0.6 * math.exp(-0.3 * layer_idx)

    x1, (w_in16, w_out16) = _ffn_split(x2d, ffn1_norm, ffn1_w_in, ffn1_w_out, tm=1024,
                                       convert=(w_in, w_out))

    qkv_a, qkv_b = _inproj(x1, mix_norm, w_in16, a_q_norm, a_k_norm, b_q_norm, b_k_norm,
                           batch=batch, seq=seq)

    a_o = _dilated(qkv_a, a_out_norm)
    lam_params = jnp.stack([lambda_q1, lambda_k1, lambda_q2, lambda_k2])
    b_o = _diff(qkv_b, lam_params, b_out_norm, lambda_init=lambda_init)

    x2 = _outproj(x1, a_o.reshape(batch * seq, GROUP_WIDTH), b_o.reshape(batch * seq, GROUP_WIDTH),
                  w_out16)
    out, _ = _ffn_split(x2, ffn2_norm, ffn2_w_in, ffn2_w_out, tm=1024)
    return out.reshape(batch, seq, d_model)


def kernel(x, ffn1_norm, ffn1_w_in, ffn1_w_out, mix_norm, w_in, a_q_norm, a_k_norm, b_q_norm, b_k_norm,
           lambda_q1, lambda_k1, lambda_q2, lambda_k2, a_out_norm, b_out_norm, w_out,
           ffn2_norm, ffn2_w_in, ffn2_w_out):
    for l in range(ffn1_norm.shape[0]):
        x = _layer(x, l, ffn1_norm[l], ffn1_w_in[l], ffn1_w_out[l], mix_norm[l], w_in[l],
                   a_q_norm[l], a_k_norm[l], b_q_norm[l], b_k_norm[l],
                   lambda_q1[l], lambda_k1[l], lambda_q2[l], lambda_k2[l],
                   a_out_norm[l], b_out_norm[l], w_out[l],
                   ffn2_norm[l], ffn2_w_in[l], ffn2_w_out[l])
    return x
```

```python
import functools
import math

import numpy as np
import jax
import jax.numpy as jnp
from jax import lax
from jax.experimental import pallas as pl
from jax.experimental.pallas import tpu as pltpu

F32 = jnp.float32
BF16 = jnp.bfloat16

HEAD_DIM = 128
N_HEADS = 8
GROUP_WIDTH = N_HEADS * HEAD_DIM
B_SUB_DIM = 64
ROPE_THETA = 500000.0
ROPE_FRACTION = 4
PATTERNS = ((128, 1), (512, 4), (2048, 16))
DIL_STEP = 4
HALF_WIN = 64
EPS = 1e-6
NEG = -1e30
LOG2E = math.log2(math.e)

Q_BLK = 128
K_BLK = Q_BLK + 2 * HALF_WIN

TILES_PER_ITER = 4
BF16_SUBLANES = 16
VMEM_LIMIT = 60 * 1024 * 1024


def _cparams(sem):
    return pltpu.CompilerParams(dimension_semantics=sem, vmem_limit_bytes=VMEM_LIMIT)


def _ffn_kernel(*refs, n_side, has_prev, emit_weights):
    x_ref, g_ref, wg_ref, wu_ref, wo_ref = refs[:5]
    pos = 5
    side_in = refs[pos:pos + n_side]
    pos += n_side + (1 if has_prev else 0)
    o_ref = refs[pos]
    side_out = refs[pos + 1:pos + 1 + n_side]
    pos += 1 + n_side
    own16 = refs[pos:pos + 3] if emit_weights else ()
    h_ref = refs[-1]
    j = pl.program_id(1)

    def step(first):
        for src, dst in zip(side_in, side_out):
            dst[...] = src[...].astype(BF16)

        weights = [w_ref[...].astype(BF16) for w_ref in (wg_ref, wu_ref, wo_ref)]
        for w16, dst in zip(weights, own16):
            dst[...] = w16
        wg, wu, wo = weights
        h = h_ref[...]
        gate = jnp.dot(h, wg, preferred_element_type=F32)
        up = jnp.dot(h, wu, preferred_element_type=F32)
        act = (gate * jax.nn.sigmoid(gate) * up * 0.5).astype(BF16)
        half_ffn = jnp.dot(act, wo, preferred_element_type=F32)
        if first:
            o_ref[...] = x_ref[...] + half_ffn
        else:
            o_ref[...] += half_ffn

    @pl.when(j == 0)
    def _():
        x = x_ref[...]
        ms = jnp.mean(x * x, axis=-1, keepdims=True)
        h_ref[...] = (x * lax.rsqrt(ms + EPS) * g_ref[...]).astype(BF16)
        step(True)

    @pl.when(j > 0)
    def _():
        step(False)


def _convert_rows(n_rows, n_steps):
    rows = BF16_SUBLANES
    while n_rows % rows or n_rows // rows > n_steps:
        rows += BF16_SUBLANES
    return rows


def _ffn(x2d, gain, wg, wu, wo, *, tm, tf, up_block_offset, first_tile, n_tiles,
         prev_out=None, convert=(), emit_weights=False):
    T, D = x2d.shape
    d_ff = wo.shape[0]
    nj = d_ff // tf
    n_steps = n_tiles * nj
    side_specs, side_shapes = [], []
    for w in convert:
        rows = _convert_rows(w.shape[0], n_steps)
        pieces = w.shape[0] // rows
        side_specs.append(pl.BlockSpec(
            (rows, w.shape[1]), lambda i, j, pieces=pieces: (jnp.minimum(i * nj + j, pieces - 1), 0)))
        side_shapes.append(jax.ShapeDtypeStruct(w.shape, BF16))
    own_specs, own_shapes = [], []
    if emit_weights:
        assert n_tiles == 1
        own_specs = [pl.BlockSpec((D, tf), lambda i, j: (0, j)), pl.BlockSpec((D, tf), lambda i, j: (0, j)),
                     pl.BlockSpec((tf, D), lambda i, j: (j, 0))]
        own_shapes = [jax.ShapeDtypeStruct((D, d_ff), BF16), jax.ShapeDtypeStruct((D, d_ff), BF16),
                      jax.ShapeDtypeStruct((d_ff, D), BF16)]
    prev_specs = [] if prev_out is None else [pl.BlockSpec(memory_space=pl.ANY)]
    prev_args = [] if prev_out is None else [prev_out]
    n_in = 5 + len(convert)
    outs = pl.pallas_call(
        functools.partial(_ffn_kernel, n_side=len(convert), has_prev=prev_out is not None,
                          emit_weights=emit_weights),
        grid=(n_tiles, nj),
        in_specs=[
            pl.BlockSpec((tm, D), lambda i, j: (i + first_tile, 0)),
            pl.BlockSpec((1, D), lambda i, j: (0, 0)),
            pl.BlockSpec((D, tf), lambda i, j: (0, j)),
            pl.BlockSpec((D, tf), lambda i, j: (0, j + up_block_offset)),
            pl.BlockSpec((tf, D), lambda i, j: (j, 0)),
        ] + side_specs + prev_specs,
        out_specs=[pl.BlockSpec((tm, D), lambda i, j: (i + first_tile, 0))] + side_specs + own_specs,
        out_shape=[jax.ShapeDtypeStruct((T, D), F32)] + side_shapes + own_shapes,
        input_output_aliases={} if prev_out is None else {n_in: 0},
        scratch_shapes=[pltpu.VMEM((tm, D), BF16)],
        compiler_params=_cparams(("parallel", "arbitrary")),
        name="ffn",
    )(x2d, gain.reshape(1, D), wg, wu, wo, *convert, *prev_args)
    n_side = len(convert)
    return outs[0], tuple(outs[1:1 + n_side]), tuple(outs[1 + n_side:])


def _ffn_split(x2d, gain, w_in, w_out, *, tm, convert=()):
    T = x2d.shape[0]
    d_ff = w_out.shape[0]
    tf_f32, tf_bf16 = 256, 512
    head, _, (wg16, wu16, wo16) = _ffn(
        x2d, gain, w_in, w_in, w_out, tm=tm, tf=tf_f32, up_block_offset=d_ff // tf_f32,
        first_tile=0, n_tiles=1, emit_weights=True)
    out, side, _ = _ffn(
        x2d, gain, wg16, wu16, wo16, tm=tm, tf=tf_bf16, up_block_offset=0,
        first_tile=1, n_tiles=T // tm - 1, prev_out=head, convert=convert)
    return out, side


def _rope_tables(seq, sub_dim):
    rd = sub_dim // ROPE_FRACTION
    half = rd // 2
    inv = ROPE_THETA ** (-np.arange(0, rd, 2, dtype=np.float64) / rd)
    ang = np.arange(seq, dtype=np.float64)[:, None] * inv[None, :]
    cos, sin = np.cos(ang), np.sin(ang)
    c = np.ones((seq, sub_dim))
    s = np.zeros((seq, sub_dim))
    c[:, :half] = cos
    c[:, half:rd] = cos
    s[:, :half] = -sin
    s[:, half:rd] = sin
    reps = HEAD_DIM // sub_dim
    return (np.tile(c, (1, reps)).astype(np.float32), np.tile(s, (1, reps)).astype(np.float32))


def _inproj_kernel(x_ref, g_ref, w_ref, cos_a_ref, sin_a_ref, cos_b_ref, sin_b_ref, gain_ref,
                   oa_ref, ob_ref):
    x = x_ref[...]
    ms = jnp.mean(x * x, axis=-1, keepdims=True)
    h = (x * lax.rsqrt(ms + EPS) * g_ref[...]).astype(BF16)

    lane = lax.broadcasted_iota(jnp.int32, (1, HEAD_DIM), 1)
    lo = lane < B_SUB_DIM
    pair = 2 * HEAD_DIM
    pairs_per_group = GROUP_WIDTH // pair

    def project(col):
        return jnp.dot(h, w_ref[:, col:col + pair], preferred_element_type=F32)

    def qk_epilogue(ph, sub_dim, gain, cos, sin):
        half = sub_dim // ROPE_FRACTION // 2
        sq = ph * ph
        if sub_dim == HEAD_DIM:
            inv = lax.rsqrt(jnp.sum(sq, axis=-1, keepdims=True) * (1.0 / sub_dim) + EPS)
        else:
            ms_lo = jnp.sum(jnp.where(lo, sq, 0.0), axis=-1, keepdims=True) * (1.0 / sub_dim)
            ms_hi = jnp.sum(jnp.where(lo, 0.0, sq), axis=-1, keepdims=True) * (1.0 / sub_dim)
            inv = jnp.where(lo, lax.rsqrt(ms_lo + EPS), lax.rsqrt(ms_hi + EPS))
        y = ph * inv * gain
        rot = jnp.where(lane % sub_dim < half, pltpu.roll(y, HEAD_DIM - half, 1), pltpu.roll(y, half, 1))
        return y * cos + rot * sin

    groups = [(0, oa_ref, 0, HEAD_DIM, 0), (1, oa_ref, 1, HEAD_DIM, 1),
              (3, ob_ref, 0, B_SUB_DIM, 2), (4, ob_ref, 1, B_SUB_DIM, 3),
              (2, oa_ref, 2, None, None), (5, ob_ref, 2, None, None)]
    steps = [(grp, pr) for grp in groups for pr in range(pairs_per_group)]
    col_of = lambda step: step[0][0] * GROUP_WIDTH + step[1] * pair
    p_next = project(col_of(steps[0]))
    for n, ((_, out_ref, slot, sub_dim, gain_row), pr) in enumerate(steps):
        p, p_next = p_next, (project(col_of(steps[n + 1])) if n + 1 < len(steps) else None)
        for e in range(2):
            ph = p[:, e * HEAD_DIM:(e + 1) * HEAD_DIM]
            if sub_dim == HEAD_DIM:
                ph = qk_epilogue(ph, sub_dim, gain_ref[gain_row], cos_a_ref[...], sin_a_ref[...])
            elif sub_dim == B_SUB_DIM:
                ph = qk_epilogue(ph, sub_dim, gain_ref[gain_row], cos_b_ref[...], sin_b_ref[...])
            out_ref[slot, 2 * pr + e] = ph.astype(out_ref.dtype)


def _inproj(x2d, mix_gain, w_in, a_q_gain, a_k_gain, b_q_gain, b_k_gain, *, batch, seq, tm=512):
    T, D = x2d.shape
    spb = seq // tm
    cos_a, sin_a = _rope_tables(seq, HEAD_DIM)
    cos_b, sin_b = _rope_tables(seq, B_SUB_DIM)
    reps = HEAD_DIM // B_SUB_DIM
    gains = jnp.stack([a_q_gain * (LOG2E * HEAD_DIM ** -0.5), a_k_gain,
                       jnp.tile(b_q_gain, reps) * (LOG2E * B_SUB_DIM ** -0.5),
                       jnp.tile(b_k_gain, reps)]).reshape(4, 1, HEAD_DIM)
    table = lambda: pl.BlockSpec((tm, HEAD_DIM), lambda i: (i % spb, 0))
    out = lambda: pl.BlockSpec((3, None, N_HEADS, tm, HEAD_DIM), lambda i: (0, i // spb, 0, i % spb, 0))
    return pl.pallas_call(
        _inproj_kernel,
        grid=(T // tm,),
        in_specs=[
            pl.BlockSpec((tm, D), lambda i: (i, 0)),
            pl.BlockSpec((1, D), lambda i: (0, 0)),
            pl.BlockSpec(w_in.shape, lambda i: (0, 0), pipeline_mode=pl.Buffered(1)),
            table(), table(), table(), table(),
            pl.BlockSpec((4, 1, HEAD_DIM), lambda i: (0, 0, 0)),
        ],
        out_specs=[out(), out()],
        out_shape=[jax.ShapeDtypeStruct((3, batch, N_HEADS, seq, HEAD_DIM), F32),
                   jax.ShapeDtypeStruct((3, batch, N_HEADS, seq, HEAD_DIM), BF16)],
        compiler_params=_cparams(("parallel",)),
        name="inproj",
    )(x2d, mix_gain.reshape(1, D), w_in, jnp.asarray(cos_a), jnp.asarray(sin_a),
      jnp.asarray(cos_b), jnp.asarray(sin_b), gains)


def _band_bias():
    col_minus_row = np.arange(K_BLK)[None, :] - np.arange(Q_BLK)[:, None]
    return np.stack([np.where(np.abs(col_minus_row - lead) <= HALF_WIN, 0.0, NEG)
                     for lead in (0, HALF_WIN, 2 * HALF_WIN)]).astype(np.float32)


def _dilated_kernel(q_ref, k_ref, v_ref, bias_ref, g_ref, o_ref,
                    qs_ref, ks_ref, vs_ref, mid_ref, og_ref, lg_ref, *, seq):
    stage_rows = 256
    srcs, dsts = (q_ref, k_ref, v_ref), (qs_ref, ks_ref, vs_ref)
    (_, dil0), (_, dil1), (_, dil2) = PATTERNS
    assert dil0 == 1 and dil1 == DIL_STEP and dil2 == DIL_STEP * dil1 and seq // dil2 == stage_rows
    sub1 = seq // dil1

    def stage0(t, carry):
        rows = pl.ds(pl.multiple_of(t * stage_rows, stage_rows), stage_rows)
        for src, dst in zip(srcs, dsts):
            dst[0, rows, :] = src[rows, :].astype(BF16)
        return carry

    def stage1(t, carry):
        per_res = sub1 // stage_rows
        r = t // per_res
        c0 = (t % per_res) * stage_rows
        rows = pl.ds(pl.multiple_of(r * sub1 + c0, stage_rows), stage_rows)
        for a, (src, dst) in enumerate(zip(srcs, dsts)):
            x = src[pl.ds(r + dil1 * c0, stage_rows, stride=dil1), :]
            mid_ref[a, rows, :] = x
            dst[1, rows, :] = x.astype(BF16)
        return carry

    def stage2(t, carry):
        r1 = t // DIL_STEP
        rr = t % DIL_STEP
        rows = pl.ds(pl.multiple_of((dil1 * rr + r1) * stage_rows, stage_rows), stage_rows)
        for a, dst in enumerate(dsts):
            dst[2, rows, :] = mid_ref[a, pl.ds(r1 * sub1 + rr, stage_rows, stride=DIL_STEP), :].astype(BF16)
        return carry

    for stage in (stage0, stage1, stage2):
        lax.fori_loop(0, seq // stage_rows, stage, 0)

    unroll = 16
    for g, (_, dil) in enumerate(PATTERNS):
        sub_len = seq // dil
        nblk = sub_len // Q_BLK

        def body(it, carry, g=g, dil=dil, sub_len=sub_len, nblk=nblk):
            kvs, outs, scores = [], [], []
            for u in range(unroll):
                t = it * unroll + u
                r = t // nblk
                m0 = (t % nblk) * Q_BLK
                k0 = jnp.clip(m0 - HALF_WIN, 0, sub_len - K_BLK)
                base = r * sub_len
                q = qs_ref[g, pl.ds(pl.multiple_of(base + m0, Q_BLK), Q_BLK), :]
                kv = pl.ds(pl.multiple_of(base + k0, HALF_WIN), K_BLK)
                s = lax.dot_general(q, ks_ref[g, kv, :], (((1,), (1,)), ((), ())),
                                    preferred_element_type=F32)
                scores.append(s + bias_ref[(m0 - k0) // HALF_WIN])
                kvs.append(kv)
                if dil == 1:
                    outs.append(pl.ds(pl.multiple_of(m0, Q_BLK), Q_BLK))
                else:
                    outs.append(pl.ds(r + dil * m0, Q_BLK, stride=dil))
            s = jnp.concatenate(scores, axis=0)
            m = jnp.max(s, axis=-1, keepdims=True)
            p = jnp.exp2(s - m)
            l = jnp.sum(p, axis=-1, keepdims=True)
            p = p.astype(BF16)
            inv_l = 1.0 / l
            lse = jnp.broadcast_to(m + jnp.log2(l), (unroll * Q_BLK, HEAD_DIM))
            for u in range(unroll):
                blk = slice(u * Q_BLK, (u + 1) * Q_BLK)
                o = jnp.dot(p[blk], vs_ref[g, kvs[u], :], preferred_element_type=F32)
                og_ref[g, outs[u], :] = o * inv_l[blk]
                lg_ref[g, outs[u], :] = lse[blk]
            return carry

        lax.fori_loop(0, dil * nblk // unroll, body, 0)

    chunk = 1024

    def comb(c, carry):
        rows = pl.ds(pl.multiple_of(c * chunk, chunk), chunk)
        l0, l1, l2 = lg_ref[0, rows, :], lg_ref[1, rows, :], lg_ref[2, rows, :]
        m = jnp.maximum(jnp.maximum(l0, l1), l2)
        w0, w1, w2 = jnp.exp2(l0 - m), jnp.exp2(l1 - m), jnp.exp2(l2 - m)
        o = (w0 * og_ref[0, rows, :] + w1 * og_ref[1, rows, :] + w2 * og_ref[2, rows, :]) / (w0 + w1 + w2)
        ms = jnp.mean(o * o, axis=-1, keepdims=True)
        o_ref[rows, :] = (o * lax.rsqrt(ms + EPS) * g_ref[...]).astype(o_ref.dtype)
        return carry

    lax.fori_loop(0, seq // chunk, comb, 0)


def _dilated(qkv, out_gain):
    _, batch, nh, seq, hd = qkv.shape
    npat = len(PATTERNS)
    spec = lambda which: pl.BlockSpec((None, None, None, seq, hd), lambda b, h: (which, b, h, 0, 0))
    return pl.pallas_call(
        functools.partial(_dilated_kernel, seq=seq),
        grid=(batch, nh),
        in_specs=[spec(0), spec(1), spec(2),
                  pl.BlockSpec((3, Q_BLK, K_BLK), lambda b, h: (0, 0, 0)),
                  pl.BlockSpec((1, hd), lambda b, h: (0, 0))],
        out_specs=pl.BlockSpec((None, seq, hd), lambda b, h: (b, 0, h)),
        out_shape=jax.ShapeDtypeStruct((batch, seq, nh * hd), BF16),
        scratch_shapes=[pltpu.VMEM((npat, seq, hd), BF16),
                        pltpu.VMEM((npat, seq, hd), BF16),
                        pltpu.VMEM((npat, seq, hd), BF16),
                        pltpu.VMEM((3, seq, hd), F32),
                        pltpu.VMEM((npat, seq, hd), F32),
                        pltpu.VMEM((npat, seq, hd), F32)],
        compiler_params=_cparams(("parallel", "parallel")),
        name="dilated",
    )(qkv, qkv, qkv, jnp.asarray(_band_bias()), out_gain.reshape(1, hd))


def _diff_kernel(lam_ref, q_ref, k_ref, v_ref, g_ref, o_ref, vaug_ref, s0_ref, *,
                 out_scale, lambda_init, tq, kc):
    seq = k_ref.shape[0]
    vaug_ref[:, :HEAD_DIM] = v_ref[...]
    vaug_ref[:, HEAD_DIM:] = jnp.ones((seq, HEAD_DIM), BF16)

    lp = lam_ref[...]
    lam = (jnp.exp(jnp.sum(lp[0:1] * lp[1:2], axis=-1, keepdims=True))
           - jnp.exp(jnp.sum(lp[2:3] * lp[3:4], axis=-1, keepdims=True)) + lambda_init)
    gain = g_ref[...] * out_scale
    lane = lax.broadcasted_iota(jnp.int32, (1, HEAD_DIM), 1)
    nc = seq // kc
    n_tiles = seq // tq

    def stacked_q(i):
        q = q_ref[pl.ds(pl.multiple_of(i * tq, tq), tq), :]
        zero = jnp.zeros_like(q)
        return jnp.concatenate([jnp.where(lane < B_SUB_DIM, q, zero),
                                jnp.where(lane < B_SUB_DIM, zero, q)], axis=0)

    def scores(q_st, c):
        return lax.dot_general(q_st, k_ref[c * kc:(c + 1) * kc, :], (((1,), (1,)), ((), ())),
                               preferred_element_type=F32)

    assert n_tiles % TILES_PER_ITER == 0
    s0_ref[...] = scores(stacked_q(0), 0)

    def one_tile(i, s_first, next_first):
        rows = pl.ds(pl.multiple_of(i * tq, tq), tq)
        q_st = stacked_q(i)
        m = acc = None
        s_next = s_first
        for c in range(nc):
            s = s_next
            s_next = scores(q_st, c + 1) if c + 1 < nc else next_first()
            m_c = jnp.max(s, axis=-1, keepdims=True)
            m_new = m_c if m is None else jnp.maximum(m, m_c)
            pv = jnp.dot(jnp.exp2(s - m_new).astype(BF16), vaug_ref[c * kc:(c + 1) * kc, :],
                         preferred_element_type=F32)
            acc = pv if m is None else acc * jnp.exp2(m - m_new) + pv
            m = m_new
        o_st = acc[:, :HEAD_DIM] / acc[:, HEAD_DIM:]
        o = o_st[:tq] - lam * o_st[tq:]
        ms = jnp.mean(o * o, axis=-1, keepdims=True)
        o_ref[rows, :] = (o * lax.rsqrt(ms + EPS) * gain).astype(o_ref.dtype)
        return s_next

    def tile_group(ii, carry):
        s_first = s0_ref[...]
        for t in range(TILES_PER_ITER):
            i = ii * TILES_PER_ITER + t
            if t + 1 < TILES_PER_ITER:
                s_first = one_tile(i, s_first, lambda i=i: scores(stacked_q(i + 1), 0))
            else:
                def hand_over(i=i):
                    s0_ref[...] = scores(stacked_q(jnp.minimum(i + 1, n_tiles - 1)), 0)
                one_tile(i, s_first, hand_over)
        return carry

    lax.fori_loop(0, n_tiles // TILES_PER_ITER, tile_group, 0)


def _diff(qkv, lam_params, out_gain, *, lambda_init, tq=512, kc=1024):
    _, batch, nh, seq, hd = qkv.shape
    kern = functools.partial(_diff_kernel, out_scale=1.0 - lambda_init, lambda_init=lambda_init,
                             tq=tq, kc=kc)
    spec = lambda which: pl.BlockSpec((None, None, None, seq, hd), lambda b, h: (which, b, h, 0, 0))
    return pl.pallas_call(
        kern,
        grid=(batch, nh),
        in_specs=[pl.BlockSpec((4, B_SUB_DIM), lambda b, h: (0, 0)),
                  spec(0), spec(1), spec(2),
                  pl.BlockSpec((1, hd), lambda b, h: (0, 0))],
        out_specs=pl.BlockSpec((None, seq, hd), lambda b, h: (b, 0, h)),
        out_shape=jax.ShapeDtypeStruct((batch, seq, nh * hd), BF16),
        scratch_shapes=[pltpu.VMEM((seq, 2 * hd), BF16), pltpu.VMEM((2 * tq, kc), F32)],
        compiler_params=_cparams(("parallel", "parallel")),
        name="diff",
    )(lam_params, qkv, qkv, qkv, out_gain.reshape(1, hd))


def _outproj_kernel(x_ref, a_ref, b_ref, wa_ref, wb_ref, o_ref):
    o_ref[...] = (x_ref[...]
                  + jnp.dot(a_ref[...], wa_ref[...], preferred_element_type=F32)
                  + jnp.dot(b_ref[...], wb_ref[...], preferred_element_type=F32))


def _outproj(x2d, a2d, b2d, w_out, *, tm=512):
    T, D = x2d.shape
    W = a2d.shape[1]
    return pl.pallas_call(
        _outproj_kernel,
        grid=(T // tm,),
        in_specs=[
            pl.BlockSpec((tm, D), lambda i: (i, 0)),
            pl.BlockSpec((tm, W), lambda i: (i, 0)),
            pl.BlockSpec((tm, W), lambda i: (i, 0)),
            pl.BlockSpec((W, D), lambda i: (0, 0)),
            pl.BlockSpec((W, D), lambda i: (1, 0)),
        ],
        out_specs=pl.BlockSpec((tm, D), lambda i: (i, 0)),
        out_shape=jax.ShapeDtypeStruct((T, D), F32),
        compiler_params=_cparams(("parallel",)),
        name="outproj",
    )(x2d, a2d, b2d, w_out, w_out)


def _layer(x, layer_idx, ffn1_norm, ffn1_w_in, ffn1_w_out, mix_norm, w_in,
           a_q_norm, a_k_norm, b_q_norm, b_k_norm,
           lambda_q1, lambda_k1, lambda_q2, lambda_k2,
           a_out_norm, b_out_norm, w_out, ffn2_norm, ffn2_w_in, ffn2_w_out):
    batch, seq, d_model = x.shape
    x2d = x.reshape(batch * seq, d_model)
    lambda_init = 0.8 - 0.6 * math.exp(-0.3 * layer_idx)

    x1, (w_in16, w_out16) = _ffn_split(x2d, ffn1_norm, ffn1_w_in, ffn1_w_out, tm=1024,
                                       convert=(w_in, w_out))

    qkv_a, qkv_b = _inproj(x1, mix_norm, w_in16, a_q_norm, a_k_norm, b_q_norm, b_k_norm,
                           batch=batch, seq=seq)

    a_o = _dilated(qkv_a, a_out_norm)
    lam_params = jnp.stack([lambda_q1, lambda_k1, lambda_q2, lambda_k2])
    b_o = _diff(qkv_b, lam_params, b_out_norm, lambda_init=lambda_init)

    x2 = _outproj(x1, a_o.reshape(batch * seq, GROUP_WIDTH), b_o.reshape(batch * seq, GROUP_WIDTH),
                  w_out16)
    out, _ = _ffn_split(x2, ffn2_norm, ffn2_w_in, ffn2_w_out, tm=1024)
    return out.reshape(batch, seq, d_model)


def kernel(x, ffn1_norm, ffn1_w_in, ffn1_w_out, mix_norm, w_in, a_q_norm, a_k_norm, b_q_norm, b_k_norm,
           lambda_q1, lambda_k1, lambda_q2, lambda_k2, a_out_norm, b_out_norm, w_out,
           ffn2_norm, ffn2_w_in, ffn2_w_out):
    for l in range(ffn1_norm.shape[0]):
        x = _layer(x, l, ffn1_norm[l], ffn1_w_in[l], ffn1_w_out[l], mix_norm[l], w_in[l],
                   a_q_norm[l], a_k_norm[l], b_q_norm[l], b_k_norm[l],
                   lambda_q1[l], lambda_k1[l], lambda_q2[l], lambda_k2[l],
                   a_out_norm[l], b_out_norm[l], w_out[l],
                   ffn2_norm[l], ffn2_w_in[l], ffn2_w_out[l])
    return x
```

```python
import functools
import math

import numpy as np
import jax
import jax.numpy as jnp
from jax import lax
from jax.experimental import pallas as pl
from jax.experimental.pallas import tpu as pltpu

F32 = jnp.float32
BF16 = jnp.bfloat16

HEAD_DIM = 128
N_HEADS = 8
GROUP_WIDTH = N_HEADS * HEAD_DIM
B_SUB_DIM = 64
ROPE_THETA = 500000.0
ROPE_FRACTION = 4
PATTERNS = ((128, 1), (512, 4), (2048, 16))
DIL_STEP = 4
HALF_WIN = 64
EPS = 1e-6
NEG = -1e30
LOG2E = math.log2(math.e)

Q_BLK = 128
K_BLK = Q_BLK + 2 * HALF_WIN

TILES_PER_ITER = 4
BF16_SUBLANES = 16
VMEM_LIMIT = 60 * 1024 * 1024


def _cparams(sem):
    return pltpu.CompilerParams(dimension_semantics=sem, vmem_limit_bytes=VMEM_LIMIT)


def _ffn_kernel(*refs, n_side, has_prev, emit_weights):
    x_ref, g_ref, wg_ref, wu_ref, wo_ref = refs[:5]
    pos = 5
    side_in = refs[pos:pos + n_side]
    pos += n_side + (1 if has_prev else 0)
    o_ref = refs[pos]
    side_out = refs[pos + 1:pos + 1 + n_side]
    pos += 1 + n_side
    own16 = refs[pos:pos + 3] if emit_weights else ()
    h_ref = refs[-1]
    j = pl.program_id(1)

    def step(first):
        for src, dst in zip(side_in, side_out):
            dst[...] = src[...].astype(BF16)

        if len(wg_ref.shape) == 2:
            gates, ups = [wg_ref[...].astype(BF16)], [wu_ref[...].astype(BF16)]
        else:
            gates = [wg_ref[k] for k in range(wg_ref.shape[0])]
            ups = [wu_ref[k] for k in range(wu_ref.shape[0])]
        wo = wo_ref[...].astype(BF16)
        for w16, dst in zip(gates + ups + [wo], own16):
            dst[...] = w16
        h = h_ref[...]
        cols = gates[0].shape[1]
        half_ffn = None
        for k, (wg, wu) in enumerate(zip(gates, ups)):
            gate = jnp.dot(h, wg, preferred_element_type=F32)
            up = jnp.dot(h, wu, preferred_element_type=F32)
            act = (gate * jax.nn.sigmoid(gate) * up * 0.5).astype(BF16)
            part = jnp.dot(act, wo[k * cols:(k + 1) * cols], preferred_element_type=F32)
            half_ffn = part if half_ffn is None else half_ffn + part
        if first:
            o_ref[...] = x_ref[...] + half_ffn
        else:
            o_ref[...] += half_ffn

    @pl.when(j == 0)
    def _():
        x = x_ref[...]
        ms = jnp.mean(x * x, axis=-1, keepdims=True)
        h_ref[...] = (x * lax.rsqrt(ms + EPS) * g_ref[...]).astype(BF16)
        step(True)

    @pl.when(j > 0)
    def _():
        step(False)


def _convert_rows(n_rows, n_steps):
    rows = BF16_SUBLANES
    while n_rows % rows or n_rows // rows > n_steps:
        rows += BF16_SUBLANES
    return rows


def _ffn(x2d, gain, wg, wu, wo, *, tm, tf, up_block_offset, first_tile, n_tiles,
         prev_out=None, convert=(), emit_weights=False):
    T, D = x2d.shape
    d_ff = wo.shape[0]
    nj = d_ff // tf
    n_steps = n_tiles * nj
    side_specs, side_shapes = [], []
    for w in convert:
        rows = _convert_rows(w.shape[0], n_steps)
        pieces = w.shape[0] // rows
        side_specs.append(pl.BlockSpec(
            (rows, w.shape[1]), lambda i, j, pieces=pieces: (jnp.minimum(i * nj + j, pieces - 1), 0)))
        side_shapes.append(jax.ShapeDtypeStruct(w.shape, BF16))
    own_specs, own_shapes = [], []
    if emit_weights:
        assert n_tiles == 1
        own_specs = [pl.BlockSpec((None, D, tf), lambda i, j: (j, 0, 0)),
                     pl.BlockSpec((None, D, tf), lambda i, j: (j, 0, 0)),
                     pl.BlockSpec((tf, D), lambda i, j: (j, 0))]
        own_shapes = [jax.ShapeDtypeStruct((nj, D, tf), BF16), jax.ShapeDtypeStruct((nj, D, tf), BF16),
                      jax.ShapeDtypeStruct((d_ff, D), BF16)]
    if wg.ndim == 2:
        gate_spec = pl.BlockSpec((D, tf), lambda i, j: (0, j))
        up_spec = pl.BlockSpec((D, tf), lambda i, j: (0, j + up_block_offset))
    else:
        per_step = tf // wg.shape[2]
        gate_spec = up_spec = pl.BlockSpec((per_step, D, wg.shape[2]), lambda i, j: (j, 0, 0))
    prev_specs = [] if prev_out is None else [pl.BlockSpec(memory_space=pl.ANY)]
    prev_args = [] if prev_out is None else [prev_out]
    n_in = 5 + len(convert)
    outs = pl.pallas_call(
        functools.partial(_ffn_kernel, n_side=len(convert), has_prev=prev_out is not None,
                          emit_weights=emit_weights),
        grid=(n_tiles, nj),
        in_specs=[
            pl.BlockSpec((tm, D), lambda i, j: (i + first_tile, 0)),
            pl.BlockSpec((1, D), lambda i, j: (0, 0)),
            gate_spec, up_spec,
            pl.BlockSpec((tf, D), lambda i, j: (j, 0)),
        ] + side_specs + prev_specs,
        out_specs=[pl.BlockSpec((tm, D), lambda i, j: (i + first_tile, 0))] + side_specs + own_specs,
        out_shape=[jax.ShapeDtypeStruct((T, D), F32)] + side_shapes + own_shapes,
        input_output_aliases={} if prev_out is None else {n_in: 0},
        scratch_shapes=[pltpu.VMEM((tm, D), BF16)],
        compiler_params=_cparams(("parallel", "arbitrary")),
        name="ffn",
    )(x2d, gain.reshape(1, D), wg, wu, wo, *convert, *prev_args)
    n_side = len(convert)
    return outs[0], tuple(outs[1:1 + n_side]), tuple(outs[1 + n_side:])


def _ffn_split(x2d, gain, w_in, w_out, *, tm, convert=()):
    T = x2d.shape[0]
    d_ff = w_out.shape[0]
    tf_f32, tf_bf16 = 256, 512
    head, _, (wg16, wu16, wo16) = _ffn(
        x2d, gain, w_in, w_in, w_out, tm=tm, tf=tf_f32, up_block_offset=d_ff // tf_f32,
        first_tile=0, n_tiles=1, emit_weights=True)
    out, side, _ = _ffn(
        x2d, gain, wg16, wu16, wo16, tm=tm, tf=tf_bf16, up_block_offset=0,
        first_tile=1, n_tiles=T // tm - 1, prev_out=head, convert=convert)
    return out, side


def _rope_tables(seq, sub_dim):
    rd = sub_dim // ROPE_FRACTION
    half = rd // 2
    inv = ROPE_THETA ** (-np.arange(0, rd, 2, dtype=np.float64) / rd)
    ang = np.arange(seq, dtype=np.float64)[:, None] * inv[None, :]
    cos, sin = np.cos(ang), np.sin(ang)
    c = np.ones((seq, sub_dim))
    s = np.zeros((seq, sub_dim))
    c[:, :half] = cos
    c[:, half:rd] = cos
    s[:, :half] = -sin
    s[:, half:rd] = sin
    reps = HEAD_DIM // sub_dim
    return (np.tile(c, (1, reps)).astype(np.float32), np.tile(s, (1, reps)).astype(np.float32))


def _inproj_kernel(x_ref, g_ref, w_ref, cos_a_ref, sin_a_ref, cos_b_ref, sin_b_ref, gain_ref,
                   oa_ref, ob_ref):
    x = x_ref[...]
    ms = jnp.mean(x * x, axis=-1, keepdims=True)
    h = (x * lax.rsqrt(ms + EPS) * g_ref[...]).astype(BF16)

    lane = lax.broadcasted_iota(jnp.int32, (1, HEAD_DIM), 1)
    lo = lane < B_SUB_DIM
    pair = 2 * HEAD_DIM
    pairs_per_group = GROUP_WIDTH // pair

    def project(col):
        return jnp.dot(h, w_ref[:, col:col + pair], preferred_element_type=F32)

    def qk_epilogue(ph, sub_dim, gain, cos, sin):
        half = sub_dim // ROPE_FRACTION // 2
        sq = ph * ph
        if sub_dim == HEAD_DIM:
            inv = lax.rsqrt(jnp.sum(sq, axis=-1, keepdims=True) * (1.0 / sub_dim) + EPS)
        else:
            ms_lo = jnp.sum(jnp.where(lo, sq, 0.0), axis=-1, keepdims=True) * (1.0 / sub_dim)
            ms_hi = jnp.sum(jnp.where(lo, 0.0, sq), axis=-1, keepdims=True) * (1.0 / sub_dim)
            inv = jnp.where(lo, lax.rsqrt(ms_lo + EPS), lax.rsqrt(ms_hi + EPS))
        y = ph * inv * gain
        rot = jnp.where(lane % sub_dim < half, pltpu.roll(y, HEAD_DIM - half, 1), pltpu.roll(y, half, 1))
        return y * cos + rot * sin

    groups = [(0, oa_ref, 0, HEAD_DIM, 0), (1, oa_ref, 1, HEAD_DIM, 1),
              (3, ob_ref, 0, B_SUB_DIM, 2), (4, ob_ref, 1, B_SUB_DIM, 3),
              (2, oa_ref, 2, None, None), (5, ob_ref, 2, None, None)]
    steps = [(grp, pr) for grp in groups for pr in range(pairs_per_group)]
    col_of = lambda step: step[0][0] * GROUP_WIDTH + step[1] * pair
    p_next = project(col_of(steps[0]))
    for n, ((_, out_ref, slot, sub_dim, gain_row), pr) in enumerate(steps):
        p, p_next = p_next, (project(col_of(steps[n + 1])) if n + 1 < len(steps) else None)
        for e in range(2):
            ph = p[:, e * HEAD_DIM:(e + 1) * HEAD_DIM]
            if sub_dim == HEAD_DIM:
                ph = qk_epilogue(ph, sub_dim, gain_ref[gain_row], cos_a_ref[...], sin_a_ref[...])
            elif sub_dim == B_SUB_DIM:
                ph = qk_epilogue(ph, sub_dim, gain_ref[gain_row], cos_b_ref[...], sin_b_ref[...])
            out_ref[slot, 2 * pr + e] = ph.astype(out_ref.dtype)


def _inproj(x2d, mix_gain, w_in, a_q_gain, a_k_gain, b_q_gain, b_k_gain, *, batch, seq, tm=512):
    T, D = x2d.shape
    spb = seq // tm
    cos_a, sin_a = _rope_tables(seq, HEAD_DIM)
    cos_b, sin_b = _rope_tables(seq, B_SUB_DIM)
    reps = HEAD_DIM // B_SUB_DIM
    gains = jnp.stack([a_q_gain * (LOG2E * HEAD_DIM ** -0.5), a_k_gain,
                       jnp.tile(b_q_gain, reps) * (LOG2E * B_SUB_DIM ** -0.5),
                       jnp.tile(b_k_gain, reps)]).reshape(4, 1, HEAD_DIM)
    table = lambda: pl.BlockSpec((tm, HEAD_DIM), lambda i: (i % spb, 0))
    out = lambda: pl.BlockSpec((3, None, N_HEADS, tm, HEAD_DIM), lambda i: (0, i // spb, 0, i % spb, 0))
    return pl.pallas_call(
        _inproj_kernel,
        grid=(T // tm,),
        in_specs=[
            pl.BlockSpec((tm, D), lambda i: (i, 0)),
            pl.BlockSpec((1, D), lambda i: (0, 0)),
            pl.BlockSpec(w_in.shape, lambda i: (0, 0), pipeline_mode=pl.Buffered(1)),
            table(), table(), table(), table(),
            pl.BlockSpec((4, 1, HEAD_DIM), lambda i: (0, 0, 0)),
        ],
        out_specs=[out(), out()],
        out_shape=[jax.ShapeDtypeStruct((3, batch, N_HEADS, seq, HEAD_DIM), F32),
                   jax.ShapeDtypeStruct((3, batch, N_HEADS, seq, HEAD_DIM), BF16)],
        compiler_params=_cparams(("parallel",)),
        name="inproj",
    )(x2d, mix_gain.reshape(1, D), w_in, jnp.asarray(cos_a), jnp.asarray(sin_a),
      jnp.asarray(cos_b), jnp.asarray(sin_b), gains)


def _band_bias():
    col_minus_row = np.arange(K_BLK)[None, :] - np.arange(Q_BLK)[:, None]
    return np.stack([np.where(np.abs(col_minus_row - lead) <= HALF_WIN, 0.0, NEG)
                     for lead in (0, HALF_WIN, 2 * HALF_WIN)]).astype(np.float32)


def _dilated_kernel(q_ref, k_ref, v_ref, bias_ref, g_ref, o_ref,
                    qs_ref, ks_ref, vs_ref, mid_ref, og_ref, lg_ref, *, seq):
    stage_rows = 256
    srcs, dsts = (q_ref, k_ref, v_ref), (qs_ref, ks_ref, vs_ref)
    (_, dil0), (_, dil1), (_, dil2) = PATTERNS
    assert dil0 == 1 and dil1 == DIL_STEP and dil2 == DIL_STEP * dil1 and seq // dil2 == stage_rows
    sub1 = seq // dil1

    def stage0(t, carry):
        rows = pl.ds(pl.multiple_of(t * stage_rows, stage_rows), stage_rows)
        for src, dst in zip(srcs, dsts):
            dst[0, rows, :] = src[rows, :].astype(BF16)
        return carry

    def stage1(t, carry):
        per_res = sub1 // stage_rows
        r = t // per_res
        c0 = (t % per_res) * stage_rows
        rows = pl.ds(pl.multiple_of(r * sub1 + c0, stage_rows), stage_rows)
        for a, (src, dst) in enumerate(zip(srcs, dsts)):
            x = src[pl.ds(r + dil1 * c0, stage_rows, stride=dil1), :]
            mid_ref[a, rows, :] = x
            dst[1, rows, :] = x.astype(BF16)
        return carry

    def stage2(t, carry):
        r1 = t // DIL_STEP
        rr = t % DIL_STEP
        rows = pl.ds(pl.multiple_of((dil1 * rr + r1) * stage_rows, stage_rows), stage_rows)
        for a, dst in enumerate(dsts):
            dst[2, rows, :] = mid_ref[a, pl.ds(r1 * sub1 + rr, stage_rows, stride=DIL_STEP), :].astype(BF16)
        return carry

    for stage in (stage0, stage1, stage2):
        lax.fori_loop(0, seq // stage_rows, stage, 0)

    unroll = 16
    for g, (_, dil) in enumerate(PATTERNS):
        sub_len = seq // dil
        nblk = sub_len // Q_BLK

        def body(it, carry, g=g, dil=dil, sub_len=sub_len, nblk=nblk):
            kvs, outs, scores = [], [], []
            for u in range(unroll):
                t = it * unroll + u
                r = t // nblk
                m0 = (t % nblk) * Q_BLK
                k0 = jnp.clip(m0 - HALF_WIN, 0, sub_len - K_BLK)
                base = r * sub_len
                q = qs_ref[g, pl.ds(pl.multiple_of(base + m0, Q_BLK), Q_BLK), :]
                kv = pl.ds(pl.multiple_of(base + k0, HALF_WIN), K_BLK)
                s = lax.dot_general(q, ks_ref[g, kv, :], (((1,), (1,)), ((), ())),
                                    preferred_element_type=F32)
                scores.append(s + bias_ref[(m0 - k0) // HALF_WIN])
                kvs.append(kv)
                if dil == 1:
                    outs.append(pl.ds(pl.multiple_of(m0, Q_BLK), Q_BLK))
                else:
                    outs.append(pl.ds(r + dil * m0, Q_BLK, stride=dil))
            s = jnp.concatenate(scores, axis=0)
            m = jnp.max(s, axis=-1, keepdims=True)
            p = jnp.exp2(s - m)
            l = jnp.sum(p, axis=-1, keepdims=True)
            p = p.astype(BF16)
            inv_l = 1.0 / l
            lse = jnp.broadcast_to(m + jnp.log2(l), (unroll * Q_BLK, HEAD_DIM))
            for u in range(unroll):
                blk = slice(u * Q_BLK, (u + 1) * Q_BLK)
                o = jnp.dot(p[blk], vs_ref[g, kvs[u], :], preferred_element_type=F32)
                og_ref[g, outs[u], :] = o * inv_l[blk]
                lg_ref[g, outs[u], :] = lse[blk]
            return carry

        lax.fori_loop(0, dil * nblk // unroll, body, 0)

    chunk = 1024

    def comb(c, carry):
        rows = pl.ds(pl.multiple_of(c * chunk, chunk), chunk)
        l0, l1, l2 = lg_ref[0, rows, :], lg_ref[1, rows, :], lg_ref[2, rows, :]
        m = jnp.maximum(jnp.maximum(l0, l1), l2)
        w0, w1, w2 = jnp.exp2(l0 - m), jnp.exp2(l1 - m), jnp.exp2(l2 - m)
        o = (w0 * og_ref[0, rows, :] + w1 * og_ref[1, rows, :] + w2 * og_ref[2, rows, :]) / (w0 + w1 + w2)
        ms = jnp.mean(o * o, axis=-1, keepdims=True)
        o_ref[rows, :] = (o * lax.rsqrt(ms + EPS) * g_ref[...]).astype(o_ref.dtype)
        return carry

    lax.fori_loop(0, seq // chunk, comb, 0)


def _dilated(qkv, out_gain):
    _, batch, nh, seq, hd = qkv.shape
    npat = len(PATTERNS)
    spec = lambda which: pl.BlockSpec((None, None, None, seq, hd), lambda b, h: (which, b, h, 0, 0))
    return pl.pallas_call(
        functools.partial(_dilated_kernel, seq=seq),
        grid=(batch, nh),
        in_specs=[spec(0), spec(1), spec(2),
                  pl.BlockSpec((3, Q_BLK, K_BLK), lambda b, h: (0, 0, 0)),
                  pl.BlockSpec((1, hd), lambda b, h: (0, 0))],
        out_specs=pl.BlockSpec((None, seq, hd), lambda b, h: (b, 0, h)),
        out_shape=jax.ShapeDtypeStruct((batch, seq, nh * hd), BF16),
        scratch_shapes=[pltpu.VMEM((npat, seq, hd), BF16),
                        pltpu.VMEM((npat, seq, hd), BF16),
                        pltpu.VMEM((npat, seq, hd), BF16),
                        pltpu.VMEM((3, seq, hd), F32),
                        pltpu.VMEM((npat, seq, hd), F32),
                        pltpu.VMEM((npat, seq, hd), F32)],
        compiler_params=_cparams(("parallel", "parallel")),
        name="dilated",
    )(qkv, qkv, qkv, jnp.asarray(_band_bias()), out_gain.reshape(1, hd))


def _diff_kernel(lam_ref, q_ref, k_ref, v_ref, g_ref, o_ref, vaug_ref, s0_ref, *,
                 out_scale, lambda_init, tq, kc):
    seq = k_ref.shape[0]
    vaug_ref[:, :HEAD_DIM] = v_ref[...]
    vaug_ref[:, HEAD_DIM:] = jnp.ones((seq, HEAD_DIM), BF16)

    lp = lam_ref[...]
    lam = (jnp.exp(jnp.sum(lp[0:1] * lp[1:2], axis=-1, keepdims=True))
           - jnp.exp(jnp.sum(lp[2:3] * lp[3:4], axis=-1, keepdims=True)) + lambda_init)
    gain = g_ref[...] * out_scale
    lane = lax.broadcasted_iota(jnp.int32, (1, HEAD_DIM), 1)
    nc = seq // kc
    n_tiles = seq // tq

    def stacked_q(i):
        q = q_ref[pl.ds(pl.multiple_of(i * tq, tq), tq), :]
        zero = jnp.zeros_like(q)
        return jnp.concatenate([jnp.where(lane < B_SUB_DIM, q, zero),
                                jnp.where(lane < B_SUB_DIM, zero, q)], axis=0)

    def scores(q_st, c):
        return lax.dot_general(q_st, k_ref[c * kc:(c + 1) * kc, :], (((1,), (1,)), ((), ())),
                               preferred_element_type=F32)

    assert n_tiles % TILES_PER_ITER == 0
    s0_ref[...] = scores(stacked_q(0), 0)

    def one_tile(i, s_first, next_first):
        rows = pl.ds(pl.multiple_of(i * tq, tq), tq)
        q_st = stacked_q(i)
        m = acc = None
        s_next = s_first
        for c in range(nc):
            s = s_next
            s_next = scores(q_st, c + 1) if c + 1 < nc else next_first()
            m_c = jnp.max(s, axis=-1, keepdims=True)
            m_new = m_c if m is None else jnp.maximum(m, m_c)
            pv = jnp.dot(jnp.exp2(s - m_new).astype(BF16), vaug_ref[c * kc:(c + 1) * kc, :],
                         preferred_element_type=F32)
            acc = pv if m is None else acc * jnp.exp2(m - m_new) + pv
            m = m_new
        o_st = acc[:, :HEAD_DIM] / acc[:, HEAD_DIM:]
        o = o_st[:tq] - lam * o_st[tq:]
        ms = jnp.mean(o * o, axis=-1, keepdims=True)
        o_ref[rows, :] = (o * lax.rsqrt(ms + EPS) * gain).astype(o_ref.dtype)
        return s_next

    def tile_group(ii, carry):
        s_first = s0_ref[...]
        for t in range(TILES_PER_ITER):
            i = ii * TILES_PER_ITER + t
            if t + 1 < TILES_PER_ITER:
                s_first = one_tile(i, s_first, lambda i=i: scores(stacked_q(i + 1), 0))
            else:
                def hand_over(i=i):
                    s0_ref[...] = scores(stacked_q(jnp.minimum(i + 1, n_tiles - 1)), 0)
                one_tile(i, s_first, hand_over)
        return carry

    lax.fori_loop(0, n_tiles // TILES_PER_ITER, tile_group, 0)


def _diff(qkv, lam_params, out_gain, *, lambda_init, tq=512, kc=1024):
    _, batch, nh, seq, hd = qkv.shape
    kern = functools.partial(_diff_kernel, out_scale=1.0 - lambda_init, lambda_init=lambda_init,
                             tq=tq, kc=kc)
    spec = lambda which: pl.BlockSpec((None, None, None, seq, hd), lambda b, h: (which, b, h, 0, 0))
    return pl.pallas_call(
        kern,
        grid=(batch, nh),
        in_specs=[pl.BlockSpec((4, B_SUB_DIM), lambda b, h: (0, 0)),
                  spec(0), spec(1), spec(2),
                  pl.BlockSpec((1, hd), lambda b, h: (0, 0))],
        out_specs=pl.BlockSpec((None, seq, hd), lambda b, h: (b, 0, h)),
        out_shape=jax.ShapeDtypeStruct((batch, seq, nh * hd), BF16),
        scratch_shapes=[pltpu.VMEM((seq, 2 * hd), BF16), pltpu.VMEM((2 * tq, kc), F32)],
        compiler_params=_cparams(("parallel", "parallel")),
        name="diff",
    )(lam_params, qkv, qkv, qkv, out_gain.reshape(1, hd))


def _outproj_kernel(x_ref, a_ref, b_ref, wa_ref, wb_ref, o_ref):
    o_ref[...] = (x_ref[...]
                  + jnp.dot(a_ref[...], wa_ref[...], preferred_element_type=F32)
                  + jnp.dot(b_ref[...], wb_ref[...], preferred_element_type=F32))


def _outproj(x2d, a2d, b2d, w_out, *, tm=512):
    T, D = x2d.shape
    W = a2d.shape[1]
    return pl.pallas_call(
        _outproj_kernel,
        grid=(T // tm,),
        in_specs=[
            pl.BlockSpec((tm, D), lambda i: (i, 0)),
            pl.BlockSpec((tm, W), lambda i: (i, 0)),
            pl.BlockSpec((tm, W), lambda i: (i, 0)),
            pl.BlockSpec((W, D), lambda i: (0, 0)),
            pl.BlockSpec((W, D), lambda i: (1, 0)),
        ],
        out_specs=pl.BlockSpec((tm, D), lambda i: (i, 0)),
        out_shape=jax.ShapeDtypeStruct((T, D), F32),
        compiler_params=_cparams(("parallel",)),
        name="outproj",
    )(x2d, a2d, b2d, w_out, w_out)


def _layer(x, layer_idx, ffn1_norm, ffn1_w_in, ffn1_w_out, mix_norm, w_in,
           a_q_norm, a_k_norm, b_q_norm, b_k_norm,
           lambda_q1, lambda_k1, lambda_q2, lambda_k2,
           a_out_norm, b_out_norm, w_out, ffn2_norm, ffn2_w_in, ffn2_w_out):
    batch, seq, d_model = x.shape
    x2d = x.reshape(batch * seq, d_model)
    lambda_init = 0.8 - 0.6 * math.exp(-0.3 * layer_idx)

    x1, (w_in16, w_out16) = _ffn_split(x2d, ffn1_norm, ffn1_w_in, ffn1_w_out, tm=1024,
                                       convert=(w_in, w_out))

    qkv_a, qkv_b = _inproj(x1, mix_norm, w_in16, a_q_norm, a_k_norm, b_q_norm, b_k_norm,
                           batch=batch, seq=seq)

    a_o = _dilated(qkv_a, a_out_norm)
    lam_params = jnp.stack([lambda_q1, lambda_k1, lambda_q2, lambda_k2])
    b_o = _diff(qkv_b, lam_params, b_out_norm, lambda_init=lambda_init)

    x2 = _outproj(x1, a_o.reshape(batch * seq, GROUP_WIDTH), b_o.reshape(batch * seq, GROUP_WIDTH),
                  w_out16)
    out, _ = _ffn_split(x2, ffn2_norm, ffn2_w_in, ffn2_w_out, tm=1024)
    return out.reshape(batch, seq, d_model)


def kernel(x, ffn1_norm, ffn1_w_in, ffn1_w_out, mix_norm, w_in, a_q_norm, a_k_norm, b_q_norm, b_k_norm,
           lambda_q1, lambda_k1, lambda_q2, lambda_k2, a_out_norm, b_out_norm, w_out,
           ffn2_norm, ffn2_w_in, ffn2_w_out):
    for l in range(ffn1_norm.shape[0]):
        x = _layer(x, l, ffn1_norm[l], ffn1_w_in[l], ffn1_w_out[l], mix_norm[l], w_in[l],
                   a_q_norm[l], a_k_norm[l], b_q_norm[l], b_k_norm[l],
                   lambda_q1[l], lambda_k1[l], lambda_q2[l], lambda_k2[l],
                   a_out_norm[l], b_out_norm[l], w_out[l],
                   ffn2_norm[l], ffn2_w_in[l], ffn2_w_out[l])
    return x
```

```python
import functools
import math

import numpy as np
import jax
import jax.numpy as jnp
from jax import lax
from jax.experimental import pallas as pl
from jax.experimental.pallas import tpu as pltpu

F32 = jnp.float32
BF16 = jnp.bfloat16

HEAD_DIM = 128
N_HEADS = 8
GROUP_WIDTH = N_HEADS * HEAD_DIM
B_SUB_DIM = 64
ROPE_THETA = 500000.0
ROPE_FRACTION = 4
PATTERNS = ((128, 1), (512, 4), (2048, 16))
DIL_STEP = 4
HALF_WIN = 64
EPS = 1e-6
NEG = -1e30
LOG2E = math.log2(math.e)

Q_BLK = 128
K_BLK = Q_BLK + 2 * HALF_WIN

TILES_PER_ITER = 4
BF16_SUBLANES = 16
VMEM_LIMIT = 60 * 1024 * 1024


def _cparams(sem):
    return pltpu.CompilerParams(dimension_semantics=sem, vmem_limit_bytes=VMEM_LIMIT)


def _ffn_kernel(*refs, n_side, has_prev, emit_weights):
    x_ref, g_ref, wg_ref, wu_ref, wo_ref = refs[:5]
    pos = 5
    side_in = refs[pos:pos + n_side]
    pos += n_side + (1 if has_prev else 0)
    o_ref = refs[pos]
    side_out = refs[pos + 1:pos + 1 + n_side]
    pos += 1 + n_side
    own16 = refs[pos:pos + 3] if emit_weights else ()
    h_ref = refs[-1]
    j = pl.program_id(1)

    def step(first):
        for src, dst in zip(side_in, side_out):
            dst[...] = src[...].astype(BF16)

        if len(wg_ref.shape) == 2:
            gates, ups = [wg_ref[...].astype(BF16)], [wu_ref[...].astype(BF16)]
        else:
            gates = [wg_ref[k] for k in range(wg_ref.shape[0])]
            ups = [wu_ref[k] for k in range(wu_ref.shape[0])]
        wo = wo_ref[...].astype(BF16)
        for w16, dst in zip(gates + ups + [wo], own16):
            dst[...] = w16
        h = h_ref[...]
        cols = gates[0].shape[1]
        half_ffn = None
        for k, (wg, wu) in enumerate(zip(gates, ups)):
            gate = jnp.dot(h, wg, preferred_element_type=F32)
            up = jnp.dot(h, wu, preferred_element_type=F32)
            act = (gate * jax.nn.sigmoid(gate) * up * 0.5).astype(BF16)
            part = jnp.dot(act, wo[k * cols:(k + 1) * cols], preferred_element_type=F32)
            half_ffn = part if half_ffn is None else half_ffn + part
        if first:
            o_ref[...] = x_ref[...] + half_ffn
        else:
            o_ref[...] += half_ffn

    @pl.when(j == 0)
    def _():
        x = x_ref[...]
        ms = jnp.mean(x * x, axis=-1, keepdims=True)
        h_ref[...] = (x * lax.rsqrt(ms + EPS) * g_ref[...]).astype(BF16)
        step(True)

    @pl.when(j > 0)
    def _():
        step(False)


def _convert_rows(n_rows, n_steps):
    rows = BF16_SUBLANES
    while n_rows % rows or n_rows // rows > n_steps:
        rows += BF16_SUBLANES
    return rows


def _ffn(x2d, gain, wg, wu, wo, *, tm, tf, up_block_offset, first_tile, n_tiles,
         prev_out=None, convert=(), emit_weights=False):
    T, D = x2d.shape
    d_ff = wo.shape[0]
    nj = d_ff // tf
    n_steps = n_tiles * nj
    side_specs, side_shapes = [], []
    for w in convert:
        rows = _convert_rows(w.shape[0], n_steps)
        pieces = w.shape[0] // rows
        side_specs.append(pl.BlockSpec(
            (rows, w.shape[1]), lambda i, j, pieces=pieces: (jnp.minimum(i * nj + j, pieces - 1), 0)))
        side_shapes.append(jax.ShapeDtypeStruct(w.shape, BF16))
    own_specs, own_shapes = [], []
    if emit_weights:
        assert n_tiles == 1
        own_specs = [pl.BlockSpec((None, D, tf), lambda i, j: (j, 0, 0)),
                     pl.BlockSpec((None, D, tf), lambda i, j: (j, 0, 0)),
                     pl.BlockSpec((tf, D), lambda i, j: (j, 0))]
        own_shapes = [jax.ShapeDtypeStruct((nj, D, tf), BF16), jax.ShapeDtypeStruct((nj, D, tf), BF16),
                      jax.ShapeDtypeStruct((d_ff, D), BF16)]
    if wg.ndim == 2:
        gate_spec = pl.BlockSpec((D, tf), lambda i, j: (0, j))
        up_spec = pl.BlockSpec((D, tf), lambda i, j: (0, j + up_block_offset))
    else:
        per_step = tf // wg.shape[2]
        gate_spec = up_spec = pl.BlockSpec((per_step, D, wg.shape[2]), lambda i, j: (j, 0, 0))
    prev_specs = [] if prev_out is None else [pl.BlockSpec(memory_space=pl.ANY)]
    prev_args = [] if prev_out is None else [prev_out]
    n_in = 5 + len(convert)
    outs = pl.pallas_call(
        functools.partial(_ffn_kernel, n_side=len(convert), has_prev=prev_out is not None,
                          emit_weights=emit_weights),
        grid=(n_tiles, nj),
        in_specs=[
            pl.BlockSpec((tm, D), lambda i, j: (i + first_tile, 0)),
            pl.BlockSpec((1, D), lambda i, j: (0, 0)),
            gate_spec, up_spec,
            pl.BlockSpec((tf, D), lambda i, j: (j, 0)),
        ] + side_specs + prev_specs,
        out_specs=[pl.BlockSpec((tm, D), lambda i, j: (i + first_tile, 0))] + side_specs + own_specs,
        out_shape=[jax.ShapeDtypeStruct((T, D), F32)] + side_shapes + own_shapes,
        input_output_aliases={} if prev_out is None else {n_in: 0},
        scratch_shapes=[pltpu.VMEM((tm, D), BF16)],
        compiler_params=_cparams(("parallel", "arbitrary")),
        name="ffn",
    )(x2d, gain.reshape(1, D), wg, wu, wo, *convert, *prev_args)
    n_side = len(convert)
    return outs[0], tuple(outs[1:1 + n_side]), tuple(outs[1 + n_side:])


def _ffn_split(x2d, gain, w_in, w_out, *, tm, convert=()):
    T = x2d.shape[0]
    d_ff = w_out.shape[0]
    tf_f32, tf_bf16 = 256, 512
    head, _, (wg16, wu16, wo16) = _ffn(
        x2d, gain, w_in, w_in, w_out, tm=tm, tf=tf_f32, up_block_offset=d_ff // tf_f32,
        first_tile=0, n_tiles=1, emit_weights=True)
    out, side, _ = _ffn(
        x2d, gain, wg16, wu16, wo16, tm=tm, tf=tf_bf16, up_block_offset=0,
        first_tile=1, n_tiles=T // tm - 1, prev_out=head, convert=convert)
    return out, side


def _rope_tables(seq, sub_dim):
    rd = sub_dim // ROPE_FRACTION
    half = rd // 2
    inv = ROPE_THETA ** (-np.arange(0, rd, 2, dtype=np.float64) / rd)
    ang = np.arange(seq, dtype=np.float64)[:, None] * inv[None, :]
    cos, sin = np.cos(ang), np.sin(ang)
    c = np.ones((seq, sub_dim))
    s = np.zeros((seq, sub_dim))
    c[:, :half] = cos
    c[:, half:rd] = cos
    s[:, :half] = -sin
    s[:, half:rd] = sin
    reps = HEAD_DIM // sub_dim
    return (np.tile(c, (1, reps)).astype(np.float32), np.tile(s, (1, reps)).astype(np.float32))


def _inproj_kernel(x_ref, g_ref, w_ref, cos_a_ref, sin_a_ref, cos_b_ref, sin_b_ref, gain_ref,
                   oa_ref, ob_ref):
    x = x_ref[...]
    ms = jnp.mean(x * x, axis=-1, keepdims=True)
    h = (x * lax.rsqrt(ms + EPS) * g_ref[...]).astype(BF16)

    lane = lax.broadcasted_iota(jnp.int32, (1, HEAD_DIM), 1)
    lo = lane < B_SUB_DIM
    pair = 2 * HEAD_DIM
    pairs_per_group = GROUP_WIDTH // pair

    def project(col):
        return jnp.dot(h, w_ref[:, col:col + pair], preferred_element_type=F32)

    def qk_epilogue(ph, sub_dim, gain, cos, sin):
        half = sub_dim // ROPE_FRACTION // 2
        sq = ph * ph
        if sub_dim == HEAD_DIM:
            inv = lax.rsqrt(jnp.sum(sq, axis=-1, keepdims=True) * (1.0 / sub_dim) + EPS)
        else:
            ms_lo = jnp.sum(jnp.where(lo, sq, 0.0), axis=-1, keepdims=True) * (1.0 / sub_dim)
            ms_hi = jnp.sum(jnp.where(lo, 0.0, sq), axis=-1, keepdims=True) * (1.0 / sub_dim)
            inv = jnp.where(lo, lax.rsqrt(ms_lo + EPS), lax.rsqrt(ms_hi + EPS))
        y = ph * inv * gain
        rot = jnp.where(lane % sub_dim < half, pltpu.roll(y, HEAD_DIM - half, 1), pltpu.roll(y, half, 1))
        return y * cos + rot * sin

    groups = [(0, oa_ref, 0, HEAD_DIM, 0), (1, oa_ref, 1, HEAD_DIM, 1),
              (3, ob_ref, 0, B_SUB_DIM, 2), (4, ob_ref, 1, B_SUB_DIM, 3),
              (2, oa_ref, 2, None, None), (5, ob_ref, 2, None, None)]
    steps = [(grp, pr) for grp in groups for pr in range(pairs_per_group)]
    col_of = lambda step: step[0][0] * GROUP_WIDTH + step[1] * pair
    p_next = project(col_of(steps[0]))
    for n, ((_, out_ref, slot, sub_dim, gain_row), pr) in enumerate(steps):
        p, p_next = p_next, (project(col_of(steps[n + 1])) if n + 1 < len(steps) else None)
        for e in range(2):
            ph = p[:, e * HEAD_DIM:(e + 1) * HEAD_DIM]
            if sub_dim == HEAD_DIM:
                ph = qk_epilogue(ph, sub_dim, gain_ref[gain_row], cos_a_ref[...], sin_a_ref[...])
            elif sub_dim == B_SUB_DIM:
                ph = qk_epilogue(ph, sub_dim, gain_ref[gain_row], cos_b_ref[...], sin_b_ref[...])
            out_ref[slot, 2 * pr + e] = ph.astype(out_ref.dtype)


def _inproj(x2d, mix_gain, w_in, a_q_gain, a_k_gain, b_q_gain, b_k_gain, *, batch, seq, tm=512):
    T, D = x2d.shape
    spb = seq // tm
    cos_a, sin_a = _rope_tables(seq, HEAD_DIM)
    cos_b, sin_b = _rope_tables(seq, B_SUB_DIM)
    reps = HEAD_DIM // B_SUB_DIM
    gains = jnp.stack([a_q_gain * (LOG2E * HEAD_DIM ** -0.5), a_k_gain,
                       jnp.tile(b_q_gain, reps) * (LOG2E * B_SUB_DIM ** -0.5),
                       jnp.tile(b_k_gain, reps)]).reshape(4, 1, HEAD_DIM)
    table = lambda: pl.BlockSpec((tm, HEAD_DIM), lambda i: (i % spb, 0))
    out = lambda: pl.BlockSpec((3, None, N_HEADS, tm, HEAD_DIM), lambda i: (0, i // spb, 0, i % spb, 0))
    return pl.pallas_call(
        _inproj_kernel,
        grid=(T // tm,),
        in_specs=[
            pl.BlockSpec((tm, D), lambda i: (i, 0)),
            pl.BlockSpec((1, D), lambda i: (0, 0)),
            pl.BlockSpec(w_in.shape, lambda i: (0, 0), pipeline_mode=pl.Buffered(1)),
            table(), table(), table(), table(),
            pl.BlockSpec((4, 1, HEAD_DIM), lambda i: (0, 0, 0)),
        ],
        out_specs=[out(), out()],
        out_shape=[jax.ShapeDtypeStruct((3, batch, N_HEADS, seq, HEAD_DIM), F32),
                   jax.ShapeDtypeStruct((3, batch, N_HEADS, seq, HEAD_DIM), BF16)],
        compiler_params=_cparams(("parallel",)),
        name="inproj",
    )(x2d, mix_gain.reshape(1, D), w_in, jnp.asarray(cos_a), jnp.asarray(sin_a),
      jnp.asarray(cos_b), jnp.asarray(sin_b), gains)


def _band_bias():
    col_minus_row = np.arange(K_BLK)[None, :] - np.arange(Q_BLK)[:, None]
    return np.stack([np.where(np.abs(col_minus_row - lead) <= HALF_WIN, 0.0, NEG)
                     for lead in (0, HALF_WIN, 2 * HALF_WIN)]).astype(np.float32)


def _dilated_kernel(q_ref, k_ref, v_ref, bias_ref, g_ref, o_ref,
                    qs_ref, ks_ref, vs_ref, mid_ref, og_ref, lg_ref, *, seq):
    stage_rows = 256
    srcs, dsts = (q_ref, k_ref, v_ref), (qs_ref, ks_ref, vs_ref)
    (_, dil0), (_, dil1), (_, dil2) = PATTERNS
    assert dil0 == 1 and dil1 == DIL_STEP and dil2 == DIL_STEP * dil1 and seq // dil2 == stage_rows
    sub1 = seq // dil1

    def stage0(t, carry):
        rows = pl.ds(pl.multiple_of(t * stage_rows, stage_rows), stage_rows)
        for src, dst in zip(srcs, dsts):
            dst[0, rows, :] = src[rows, :].astype(BF16)
        return carry

    def stage1(t, carry):
        per_res = sub1 // stage_rows
        r = t // per_res
        c0 = (t % per_res) * stage_rows
        rows = pl.ds(pl.multiple_of(r * sub1 + c0, stage_rows), stage_rows)
        for a, (src, dst) in enumerate(zip(srcs, dsts)):
            x = src[pl.ds(r + dil1 * c0, stage_rows, stride=dil1), :]
            mid_ref[a, rows, :] = x
            dst[1, rows, :] = x.astype(BF16)
        return carry

    def stage2(t, carry):
        r1 = t // DIL_STEP
        rr = t % DIL_STEP
        rows = pl.ds(pl.multiple_of((dil1 * rr + r1) * stage_rows, stage_rows), stage_rows)
        for a, dst in enumerate(dsts):
            dst[2, rows, :] = mid_ref[a, pl.ds(r1 * sub1 + rr, stage_rows, stride=DIL_STEP), :].astype(BF16)
        return carry

    for stage in (stage0, stage1, stage2):
        lax.fori_loop(0, seq // stage_rows, stage, 0)

    unroll = 16
    for g, (_, dil) in enumerate(PATTERNS):
        sub_len = seq // dil
        nblk = sub_len // Q_BLK

        def body(it, carry, g=g, dil=dil, sub_len=sub_len, nblk=nblk):
            kvs, outs, scores = [], [], []
            for u in range(unroll):
                t = it * unroll + u
                r = t // nblk
                m0 = (t % nblk) * Q_BLK
                k0 = jnp.clip(m0 - HALF_WIN, 0, sub_len - K_BLK)
                base = r * sub_len
                q = qs_ref[g, pl.ds(pl.multiple_of(base + m0, Q_BLK), Q_BLK), :]
                kv = pl.ds(pl.multiple_of(base + k0, HALF_WIN), K_BLK)
                s = lax.dot_general(q, ks_ref[g, kv, :], (((1,), (1,)), ((), ())),
                                    preferred_element_type=F32)
                scores.append(s + bias_ref[(m0 - k0) // HALF_WIN])
                kvs.append(kv)
                if dil == 1:
                    outs.append(pl.ds(pl.multiple_of(m0, Q_BLK), Q_BLK))
                else:
                    outs.append(pl.ds(r + dil * m0, Q_BLK, stride=dil))
            s = jnp.concatenate(scores, axis=0)
            m = jnp.max(s, axis=-1, keepdims=True)
            p = jnp.exp2(s - m)
            l = jnp.sum(p, axis=-1, keepdims=True)
            p = p.astype(BF16)
            inv_l = 1.0 / l
            lse = jnp.broadcast_to(m + jnp.log2(l), (unroll * Q_BLK, HEAD_DIM))
            for u in range(unroll):
                blk = slice(u * Q_BLK, (u + 1) * Q_BLK)
                o = jnp.dot(p[blk], vs_ref[g, kvs[u], :], preferred_element_type=F32)
                og_ref[g, outs[u], :] = o * inv_l[blk]
                lg_ref[g, outs[u], :] = lse[blk]
            return carry

        lax.fori_loop(0, dil * nblk // unroll, body, 0)

    chunk = 1024

    def comb(c, carry):
        rows = pl.ds(pl.multiple_of(c * chunk, chunk), chunk)
        l0, l1, l2 = lg_ref[0, rows, :], lg_ref[1, rows, :], lg_ref[2, rows, :]
        m = jnp.maximum(jnp.maximum(l0, l1), l2)
        w0, w1, w2 = jnp.exp2(l0 - m), jnp.exp2(l1 - m), jnp.exp2(l2 - m)
        o = (w0 * og_ref[0, rows, :] + w1 * og_ref[1, rows, :] + w2 * og_ref[2, rows, :]) / (w0 + w1 + w2)
        ms = jnp.mean(o * o, axis=-1, keepdims=True)
        o_ref[rows, :] = (o * lax.rsqrt(ms + EPS) * g_ref[...]).astype(o_ref.dtype)
        return carry

    lax.fori_loop(0, seq // chunk, comb, 0)


def _dilated(qkv, out_gain):
    _, batch, nh, seq, hd = qkv.shape
    npat = len(PATTERNS)
    spec = lambda which: pl.BlockSpec((None, None, None, seq, hd), lambda b, h: (which, b, h, 0, 0))
    return pl.pallas_call(
        functools.partial(_dilated_kernel, seq=seq),
        grid=(batch, nh),
        in_specs=[spec(0), spec(1), spec(2),
                  pl.BlockSpec((3, Q_BLK, K_BLK), lambda b, h: (0, 0, 0)),
                  pl.BlockSpec((1, hd), lambda b, h: (0, 0))],
        out_specs=pl.BlockSpec((None, seq, hd), lambda b, h: (b, 0, h)),
        out_shape=jax.ShapeDtypeStruct((batch, seq, nh * hd), BF16),
        scratch_shapes=[pltpu.VMEM((npat, seq, hd), BF16),
                        pltpu.VMEM((npat, seq, hd), BF16),
                        pltpu.VMEM((npat, seq, hd), BF16),
                        pltpu.VMEM((3, seq, hd), F32),
                        pltpu.VMEM((npat, seq, hd), F32),
                        pltpu.VMEM((npat, seq, hd), F32)],
        compiler_params=_cparams(("parallel", "parallel")),
        name="dilated",
    )(qkv, qkv, qkv, jnp.asarray(_band_bias()), out_gain.reshape(1, hd))


def _diff_kernel(lam_ref, q_ref, k_ref, v_ref, g_ref, win_ref, wout_ref,
                 o_ref, wg16_ref, wu16_ref, wo16_ref, vaug_ref, s0_ref, *,
                 out_scale, lambda_init, tq, kc):
    seq = k_ref.shape[0]
    vaug_ref[:, :HEAD_DIM] = v_ref[...]
    vaug_ref[:, HEAD_DIM:] = jnp.ones((seq, HEAD_DIM), BF16)

    lp = lam_ref[...]
    lam = (jnp.exp(jnp.sum(lp[0:1] * lp[1:2], axis=-1, keepdims=True))
           - jnp.exp(jnp.sum(lp[2:3] * lp[3:4], axis=-1, keepdims=True)) + lambda_init)
    gain = g_ref[...] * out_scale
    lane = lax.broadcasted_iota(jnp.int32, (1, HEAD_DIM), 1)
    nc = seq // kc
    n_tiles = seq // tq

    def stacked_q(i):
        q = q_ref[pl.ds(pl.multiple_of(i * tq, tq), tq), :]
        zero = jnp.zeros_like(q)
        return jnp.concatenate([jnp.where(lane < B_SUB_DIM, q, zero),
                                jnp.where(lane < B_SUB_DIM, zero, q)], axis=0)

    def scores(q_st, c):
        return lax.dot_general(q_st, k_ref[c * kc:(c + 1) * kc, :], (((1,), (1,)), ((), ())),
                               preferred_element_type=F32)

    assert n_tiles % TILES_PER_ITER == 0
    s0_ref[...] = scores(stacked_q(0), 0)

    def one_tile(i, s_first, next_first):
        rows = pl.ds(pl.multiple_of(i * tq, tq), tq)
        q_st = stacked_q(i)
        m = acc = None
        s_next = s_first
        for c in range(nc):
            s = s_next
            s_next = scores(q_st, c + 1) if c + 1 < nc else next_first()
            m_c = jnp.max(s, axis=-1, keepdims=True)
            m_new = m_c if m is None else jnp.maximum(m, m_c)
            pv = jnp.dot(jnp.exp2(s - m_new).astype(BF16), vaug_ref[c * kc:(c + 1) * kc, :],
                         preferred_element_type=F32)
            acc = pv if m is None else acc * jnp.exp2(m - m_new) + pv
            m = m_new
        o_st = acc[:, :HEAD_DIM] / acc[:, HEAD_DIM:]
        o = o_st[:tq] - lam * o_st[tq:]
        ms = jnp.mean(o * o, axis=-1, keepdims=True)
        o_ref[rows, :] = (o * lax.rsqrt(ms + EPS) * gain).astype(o_ref.dtype)
        return s_next

    n_iters = n_tiles // TILES_PER_ITER
    d_ff = win_ref.shape[1] // 2
    rows_in, rows_out = win_ref.shape[0] // n_iters, wout_ref.shape[0] // n_iters

    def round_ffn_weights(ii):
        r = pl.ds(pl.multiple_of(ii * rows_in, rows_in), rows_in)
        for kb in range(wg16_ref.shape[0]):
            cols = wg16_ref.shape[2]
            wg16_ref[kb, r, :] = win_ref[r, kb * cols:(kb + 1) * cols].astype(BF16)
            wu16_ref[kb, r, :] = win_ref[r, d_ff + kb * cols:d_ff + (kb + 1) * cols].astype(BF16)
        ro = pl.ds(pl.multiple_of(ii * rows_out, BF16_SUBLANES), rows_out)
        wo16_ref[ro, :] = wout_ref[ro, :].astype(BF16)

    def tile_group(ii, carry):
        s_first = s0_ref[...]
        for t in range(TILES_PER_ITER):
            i = ii * TILES_PER_ITER + t
            if t == 1:
                round_ffn_weights(ii)
            if t + 1 < TILES_PER_ITER:
                s_first = one_tile(i, s_first, lambda i=i: scores(stacked_q(i + 1), 0))
            else:
                def hand_over(i=i):
                    s0_ref[...] = scores(stacked_q(jnp.minimum(i + 1, n_tiles - 1)), 0)
                one_tile(i, s_first, hand_over)
        return carry

    lax.fori_loop(0, n_tiles // TILES_PER_ITER, tile_group, 0)


def _diff(qkv, lam_params, out_gain, ffn_w_in, ffn_w_out, *, lambda_init, tq=512, kc=1024, ffn_cols=256):
    _, batch, nh, seq, hd = qkv.shape
    d_model, d_ff = ffn_w_out.shape[1], ffn_w_out.shape[0]
    steps = batch * nh
    rows_in, rows_out, blocks = d_model // steps, d_ff // steps, d_ff // ffn_cols
    assert rows_out % (2 * BF16_SUBLANES) == 0 and rows_in % (2 * BF16_SUBLANES) == 0
    kern = functools.partial(_diff_kernel, out_scale=1.0 - lambda_init, lambda_init=lambda_init,
                             tq=tq, kc=kc)
    spec = lambda which: pl.BlockSpec((None, None, None, seq, hd), lambda b, h: (which, b, h, 0, 0))
    stack = lambda: pl.BlockSpec((blocks, rows_in, ffn_cols), lambda b, h: (0, b * nh + h, 0))
    outs = pl.pallas_call(
        kern,
        grid=(batch, nh),
        in_specs=[pl.BlockSpec((4, B_SUB_DIM), lambda b, h: (0, 0)),
                  spec(0), spec(1), spec(2),
                  pl.BlockSpec((1, hd), lambda b, h: (0, 0)),
                  pl.BlockSpec((rows_in, 2 * d_ff), lambda b, h: (b * nh + h, 0)),
                  pl.BlockSpec((rows_out, d_model), lambda b, h: (b * nh + h, 0))],
        out_specs=[pl.BlockSpec((None, seq, hd), lambda b, h: (b, 0, h)), stack(), stack(),
                   pl.BlockSpec((rows_out, d_model), lambda b, h: (b * nh + h, 0))],
        out_shape=[jax.ShapeDtypeStruct((batch, seq, nh * hd), BF16),
                   jax.ShapeDtypeStruct((blocks, d_model, ffn_cols), BF16),
                   jax.ShapeDtypeStruct((blocks, d_model, ffn_cols), BF16),
                   jax.ShapeDtypeStruct((d_ff, d_model), BF16)],
        scratch_shapes=[pltpu.VMEM((seq, 2 * hd), BF16), pltpu.VMEM((2 * tq, kc), F32)],
        compiler_params=_cparams(("parallel", "parallel")),
        name="diff",
    )(lam_params, qkv, qkv, qkv, out_gain.reshape(1, hd), ffn_w_in, ffn_w_out)
    return outs[0], tuple(outs[1:])


def _outproj_kernel(x_ref, a_ref, b_ref, wa_ref, wb_ref, o_ref):
    o_ref[...] = (x_ref[...]
                  + jnp.dot(a_ref[...], wa_ref[...], preferred_element_type=F32)
                  + jnp.dot(b_ref[...], wb_ref[...], preferred_element_type=F32))


def _outproj(x2d, a2d, b2d, w_out, *, tm=512):
    T, D = x2d.shape
    W = a2d.shape[1]
    return pl.pallas_call(
        _outproj_kernel,
        grid=(T // tm,),
        in_specs=[
            pl.BlockSpec((tm, D), lambda i: (i, 0)),
            pl.BlockSpec((tm, W), lambda i: (i, 0)),
            pl.BlockSpec((tm, W), lambda i: (i, 0)),
            pl.BlockSpec((W, D), lambda i: (0, 0)),
            pl.BlockSpec((W, D), lambda i: (1, 0)),
        ],
        out_specs=pl.BlockSpec((tm, D), lambda i: (i, 0)),
        out_shape=jax.ShapeDtypeStruct((T, D), F32),
        compiler_params=_cparams(("parallel",)),
        name="outproj",
    )(x2d, a2d, b2d, w_out, w_out)


def _layer(x, layer_idx, ffn1_norm, ffn1_w_in, ffn1_w_out, mix_norm, w_in,
           a_q_norm, a_k_norm, b_q_norm, b_k_norm,
           lambda_q1, lambda_k1, lambda_q2, lambda_k2,
           a_out_norm, b_out_norm, w_out, ffn2_norm, ffn2_w_in, ffn2_w_out):
    batch, seq, d_model = x.shape
    x2d = x.reshape(batch * seq, d_model)
    lambda_init = 0.8 - 0.6 * math.exp(-0.3 * layer_idx)

    x1, (w_in16, w_out16) = _ffn_split(x2d, ffn1_norm, ffn1_w_in, ffn1_w_out, tm=1024,
                                       convert=(w_in, w_out))

    qkv_a, qkv_b = _inproj(x1, mix_norm, w_in16, a_q_norm, a_k_norm, b_q_norm, b_k_norm,
                           batch=batch, seq=seq)

    a_o = _dilated(qkv_a, a_out_norm)
    lam_params = jnp.stack([lambda_q1, lambda_k1, lambda_q2, lambda_k2])
    b_o, (wg16, wu16, wo16) = _diff(qkv_b, lam_params, b_out_norm, ffn2_w_in, ffn2_w_out,
                                    lambda_init=lambda_init)

    x2 = _outproj(x1, a_o.reshape(batch * seq, GROUP_WIDTH), b_o.reshape(batch * seq, GROUP_WIDTH),
                  w_out16)
    out, _, _ = _ffn(x2, ffn2_norm, wg16, wu16, wo16, tm=1024, tf=512, up_block_offset=0,
                     first_tile=0, n_tiles=batch * seq // 1024)
    return out.reshape(batch, seq, d_model)


def kernel(x, ffn1_norm, ffn1_w_in, ffn1_w_out, mix_norm, w_in, a_q_norm, a_k_norm, b_q_norm, b_k_norm,
           lambda_q1, lambda_k1, lambda_q2, lambda_k2, a_out_norm, b_out_norm, w_out,
           ffn2_norm, ffn2_w_in, ffn2_w_out):
    for l in range(ffn1_norm.shape[0]):
        x = _layer(x, l, ffn1_norm[l], ffn1_w_in[l], ffn1_w_out[l], mix_norm[l], w_in[l],
                   a_q_norm[l], a_k_norm[l], b_q_norm[l], b_k_norm[l],
                   lambda_q1[l], lambda_k1[l], lambda_q2[l], lambda_k2[l],
                   a_out_norm[l], b_out_norm[l], w_out[l],
                   ffn2_norm[l], ffn2_w_in[l], ffn2_w_out[l])
    return x
```
